```python
import math
import jax, jax.numpy as jnp
from jax import lax
import numpy as np

D_MODEL = 2048
BATCH = 2
SEQ = 8192
DEPTH = 4

BLOCK = 128
RET_HEADS = 4
RET_QK_DIM = 256
RET_V_DIM = 512
SG_GROUPS = 4
SG_GROUP_DIM = 256
ATT_HEADS = 8
ATT_HEAD_DIM = 128
DILATION_CONFIGS = ((128, 1), (512, 4), (2048, 16))
REL_BUCKETS = 32
REL_MAX_DIST = 2048
D_FF = 5632
ROPE_BASE = 10000.0
EPS = 1e-6
NEG_INF = -1e30

RET_QK_W = RET_HEADS * RET_QK_DIM
RET_V_W = RET_HEADS * RET_V_DIM
SG_W = SG_GROUPS * SG_GROUP_DIM
ATT_W = ATT_HEADS * ATT_HEAD_DIM
N_BRANCH = 3
IN_SPLITS = (RET_QK_W, RET_QK_W, RET_V_W, RET_V_W, SG_W, SG_W, ATT_W, ATT_W, ATT_W, N_BRANCH * D_MODEL)
D_IN = sum(IN_SPLITS)

kernel_name = 'hybrid_retention_sgu_dilated_attn_macaron'

f32 = jnp.float32


def _rmsnorm(x, g):
    xf = x.astype(f32)
    y = xf * lax.rsqrt(jnp.mean(xf * xf, axis=-1, keepdims=True) + EPS) * g.astype(f32)
    return y.astype(x.dtype)


def _layernorm(x, g, b):
    xf = x.astype(f32)
    mu = jnp.mean(xf, axis=-1, keepdims=True)
    var = jnp.mean(jnp.square(xf - mu), axis=-1, keepdims=True)
    y = (xf - mu) * lax.rsqrt(var + EPS) * g.astype(f32) + b.astype(f32)
    return y.astype(x.dtype)


def _swiglu(h, w_gate, w_up, w_down):
    return (jax.nn.silu(h @ w_gate) * (h @ w_up)) @ w_down


def _rotary(t, positions):
    dh = t.shape[-1]
    inv = ROPE_BASE ** (-jnp.arange(0, dh, 2, dtype=f32) / dh)
    ang = positions.astype(f32)[:, None] * inv[None, :]
    cos, sin = jnp.cos(ang), jnp.sin(ang)
    t1, t2 = t[..., : dh // 2], t[..., dh // 2:]
    return jnp.concatenate([t1 * cos - t2 * sin, t1 * sin + t2 * cos], axis=-1)


def _retention(q, k, v):
    B, H, S, dk = q.shape
    dv = v.shape[-1]
    C = BLOCK
    nc = S // C
    log_g = jnp.log1p(-(2.0 ** (-5.0 - jnp.arange(H, dtype=f32))))
    idx = jnp.arange(C, dtype=f32)
    rel = idx[:, None] - idx[None, :]
    inner = jnp.where(rel[None] >= 0, jnp.exp(jnp.maximum(rel, 0.0)[None] * log_g[:, None, None]), 0.0)
    xi = jnp.exp((idx + 1.0)[None, :] * log_g[:, None])
    zeta = jnp.exp((C - 1.0 - idx)[None, :] * log_g[:, None])
    g_chunk = jnp.exp(C * log_g)
    k = k * (dk ** -0.5)

    def chunks(t):
        return jnp.moveaxis(t.reshape(B, H, nc, C, t.shape[-1]), 2, 0)

    def step(R, qkv):
        qc, kc, vc = qkv
        a = jnp.einsum('bhid,bhjd->bhij', qc, kc) * inner
        o = jnp.einsum('bhij,bhje->bhie', a, vc) + jnp.einsum('bhid,bhde->bhie', qc, R) * xi[..., None]
        R = R * g_chunk[:, None, None] + jnp.einsum('bhjd,bhje->bhde', kc * zeta[..., None], vc)
        return R, o

    R0 = jnp.zeros((B, H, dk, dv), f32)
    _, o = lax.scan(step, R0, (chunks(q), chunks(k), chunks(v)))
    return jnp.moveaxis(o, 0, 2).reshape(B, H, S, dv)


def _t5_bucket(dist):
    max_exact = REL_BUCKETS // 2
    d_f = jnp.maximum(dist, 1).astype(f32)
    large = max_exact + (jnp.log(d_f / max_exact) / math.log(REL_MAX_DIST / max_exact)
                         * (REL_BUCKETS - max_exact)).astype(jnp.int32)
    large = jnp.minimum(large, REL_BUCKETS - 1)
    return jnp.where(dist < max_exact, dist, large)


def _dilated_branch(q, k, v, bias_tab, window, dilation):
    B, H, S, hd = q.shape
    n_back = window // dilation
    blk = n_back
    span = dilation * blk
    L = -(-S // span) * span
    nb = L // span

    def to_stream(t):
        t = jnp.pad(t, ((0, 0), (0, 0), (0, L - S), (0, 0))).reshape(B, H, L // dilation, dilation, hd)
        return jnp.swapaxes(t, 2, 3).reshape(B, H, dilation, nb, blk, hd)

    def with_prev(t):
        prev = jnp.pad(t[:, :, :, :-1], ((0, 0), (0, 0), (0, 0), (1, 0), (0, 0), (0, 0)))
        return jnp.concatenate([prev, t], axis=4)

    def from_stream(t):
        t = t.reshape(B, H, dilation, L // dilation, t.shape[-1])
        return jnp.swapaxes(t, 2, 3).reshape(B, H, L, t.shape[-1])[:, :, :S]

    qs = to_stream(q)
    kc = with_prev(to_stream(k))
    vc = with_prev(to_stream(v))
    i = jnp.arange(blk)[:, None]
    j = jnp.arange(2 * blk)[None, :]
    steps = blk + i - j
    band = (steps >= 0) & (steps <= n_back)
    not_before_start = (jnp.arange(nb)[:, None, None] > 0) | (j[None] >= blk)
    mask = band[None] & not_before_start
    bucket = _t5_bucket(dilation * jnp.maximum(steps, 0))
    bias = jnp.transpose(bias_tab[bucket], (2, 0, 1))

    logits = jnp.einsum('bhrnqd,bhrnkd->bhrnqk', qs, kc) * (hd ** -0.5) + bias[:, None, None]
    logits = jnp.where(mask, logits, NEG_INF)
    m = jnp.max(logits, axis=-1, keepdims=True)
    p = jnp.exp(logits - m)
    den = jnp.sum(p, axis=-1, keepdims=True)
    o = jnp.einsum('bhrnqk,bhrnkd->bhrnqd', p, vc) / den
    lse = m + jnp.log(den)
    return from_stream(o), from_stream(lse)


def _mixer(h, w_in, b_gate, sg_ln_g, sg_ln_b, sg_w, sg_b, rel_bias,
           w_proj_ret, w_proj_sg, w_proj_att, w_out):
    B, S, _ = h.shape
    z = h @ w_in
    offs = [int(o) for o in np.cumsum(IN_SPLITS)[:-1]]
    rq, rk, rv, rg, su, sv, aq, ak, av, gl = jnp.split(z, offs, axis=-1)

    def heads(t, n):
        return jnp.swapaxes(t.reshape(B, S, n, -1), 1, 2).astype(f32)

    def merge_heads(t):
        return jnp.swapaxes(t, 1, 2).reshape(B, S, -1).astype(h.dtype)

    pos = jnp.arange(S)
    ret = _retention(_rotary(heads(rq, RET_HEADS), pos), _rotary(heads(rk, RET_HEADS), pos), heads(rv, RET_HEADS))
    ret = ret * lax.rsqrt(jnp.mean(ret * ret, axis=-1, keepdims=True) + EPS)
    y_ret = (jax.nn.silu(rg) * merge_heads(ret)) @ w_proj_ret

    su = jax.nn.gelu(su)
    sv = _layernorm(jax.nn.gelu(sv), sg_ln_g, sg_ln_b)
    vch = sv.reshape(B, S // BLOCK, BLOCK, SG_GROUPS, SG_GROUP_DIM)
    w_causal = sg_w * jnp.tril(jnp.ones((BLOCK, BLOCK), sg_w.dtype))
    s_mix = jnp.einsum('gts,bnsgc->bntgc', w_causal, vch) + sg_b.T[:, :, None]
    y_sg = (su * s_mix.reshape(B, S, SG_W)) @ w_proj_sg

    qa, ka, va = heads(aq, ATT_HEADS), heads(ak, ATT_HEADS), heads(av, ATT_HEADS)
    bias_tab = rel_bias.astype(f32)
    outs, lses = [], []
    for window, dil in DILATION_CONFIGS:
        o, l = _dilated_branch(qa, ka, va, bias_tab, window, dil)
        outs.append(o)
        lses.append(l)
    wts = jax.nn.softmax(jnp.stack(lses), axis=0)
    att = jnp.sum(wts * jnp.stack(outs), axis=0)
    y_att = merge_heads(att) @ w_proj_att

    gates = jax.nn.sigmoid((gl + b_gate).astype(f32)).astype(h.dtype).reshape(B, S, N_BRANCH, D_MODEL)
    merged = gates[:, :, 0] * y_ret + gates[:, :, 1] * y_sg + gates[:, :, 2] * y_att
    return merged @ w_out


def setup_inputs(seed: int = 0) -> dict:
    key = jax.random.key(seed)
    ks = jax.random.split(key, 24)

    def nrm(k, shape, scale):
        return jax.random.normal(k, shape, f32) * scale

    def gain(k, shape):
        return 1.0 + 0.02 * jax.random.normal(k, shape, f32)

    return {
        'x': nrm(ks[0], (BATCH, SEQ, D_MODEL), 1.0),
        'ffn1_norm': gain(ks[1], (DEPTH, D_MODEL)),
        'ffn1_w_gate': nrm(ks[2], (DEPTH, D_MODEL, D_FF), D_MODEL ** -0.5),
        'ffn1_w_up': nrm(ks[3], (DEPTH, D_MODEL, D_FF), D_MODEL ** -0.5),
        'ffn1_w_down': nrm(ks[4], (DEPTH, D_FF, D_MODEL), D_FF ** -0.5),
        'mix_norm': gain(ks[5], (DEPTH, D_MODEL)),
        'w_in': nrm(ks[6], (DEPTH, D_MODEL, D_IN), D_MODEL ** -0.5),
        'b_gate': nrm(ks[7], (DEPTH, N_BRANCH * D_MODEL), 0.02),
        'sg_ln_g': gain(ks[8], (DEPTH, SG_W)),
        'sg_ln_b': nrm(ks[9], (DEPTH, SG_W), 0.02),
        'sg_w': nrm(ks[10], (DEPTH, SG_GROUPS, BLOCK, BLOCK), BLOCK ** -0.5),
        'sg_b': nrm(ks[11], (DEPTH, SG_GROUPS, BLOCK), 0.02),
        'rel_bias': nrm(ks[12], (REL_BUCKETS, ATT_HEADS), 0.5),
        'w_proj_ret': nrm(ks[13], (DEPTH, RET_V_W, D_MODEL), RET_V_W ** -0.5),
        'w_proj_sg': nrm(ks[14], (DEPTH, SG_W, D_MODEL), SG_W ** -0.5),
        'w_proj_att': nrm(ks[15], (DEPTH, ATT_W, D_MODEL), ATT_W ** -0.5),
        'w_out': nrm(ks[16], (DEPTH, D_MODEL, D_MODEL), D_MODEL ** -0.5),
        'ffn2_norm': gain(ks[17], (DEPTH, D_MODEL)),
        'ffn2_w_gate': nrm(ks[18], (DEPTH, D_MODEL, D_FF), D_MODEL ** -0.5),
        'ffn2_w_up': nrm(ks[19], (DEPTH, D_MODEL, D_FF), D_MODEL ** -0.5),
        'ffn2_w_down': nrm(ks[20], (DEPTH, D_FF, D_MODEL), D_FF ** -0.5),
        'final_norm': gain(ks[21], (D_MODEL,)),
    }


def reference(x, ffn1_norm, ffn1_w_gate, ffn1_w_up, ffn1_w_down, mix_norm, w_in, b_gate,
              sg_ln_g, sg_ln_b, sg_w, sg_b, rel_bias, w_proj_ret, w_proj_sg, w_proj_att, w_out,
              ffn2_norm, ffn2_w_gate, ffn2_w_up, ffn2_w_down, final_norm):
    for l in range(DEPTH):
        x = x + 0.5 * _swiglu(_rmsnorm(x, ffn1_norm[l]), ffn1_w_gate[l], ffn1_w_up[l], ffn1_w_down[l])
        x = x + _mixer(_rmsnorm(x, mix_norm[l]), w_in[l], b_gate[l], sg_ln_g[l], sg_ln_b[l],
                       sg_w[l], sg_b[l], rel_bias, w_proj_ret[l], w_proj_sg[l], w_proj_att[l], w_out[l])
        x = x + 0.5 * _swiglu(_rmsnorm(x, ffn2_norm[l]), ffn2_w_gate[l], ffn2_w_up[l], ffn2_w_down[l])
    return _rmsnorm(x, final_norm)
```

```python
import functools
import math

import jax
import jax.numpy as jnp
from jax import lax
from jax.experimental import pallas as pl
from jax.experimental.pallas import tpu as pltpu

f32 = jnp.float32
bf16 = jnp.bfloat16

D_MODEL = 2048
DEPTH = 4
BLOCK = 128
RET_HEADS = 4
RET_QK_DIM = 256
RET_V_DIM = 512
SG_GROUPS = 4
SG_GROUP_DIM = 256
ATT_HEADS = 8
ATT_HEAD_DIM = 128
DILATION_CONFIGS = ((128, 1), (512, 4), (2048, 16))
REL_BUCKETS = 32
REL_MAX_DIST = 2048
D_FF = 5632
ROPE_BASE = 10000.0
EPS = 1e-6
NEG_INF = -1e30

RET_QK_W = RET_HEADS * RET_QK_DIM
RET_V_W = RET_HEADS * RET_V_DIM
SG_W = SG_GROUPS * SG_GROUP_DIM
ATT_W = ATT_HEADS * ATT_HEAD_DIM
N_BRANCH = 3
Z_W = 2 * RET_QK_W + 2 * RET_V_W + 2 * SG_W + 3 * ATT_W
GATE_W = N_BRANCH * D_MODEL

ATT_SUPER = 2048
V7X_VMEM_LIMIT = 60 * 1024 * 1024


def _params(sem, vmem_mb=None):
    return pltpu.CompilerParams(
        dimension_semantics=sem,
        vmem_limit_bytes=None if vmem_mb is None else min(vmem_mb * 1024 * 1024, V7X_VMEM_LIMIT))


def _rms(x, g):
    return x * lax.rsqrt(jnp.mean(x * x, axis=-1, keepdims=True) + EPS) * g


def _rmsnorm_kernel(x_ref, g_ref, o_ref):
    o_ref[...] = _rms(x_ref[...], g_ref[...]).astype(o_ref.dtype)


def rmsnorm(x, g, out_dtype, tm=512):
    n, d = x.shape
    return pl.pallas_call(
        _rmsnorm_kernel,
        grid=(n // tm,),
        in_specs=[pl.BlockSpec((tm, d), lambda i: (i, 0)),
                  pl.BlockSpec((1, d), lambda i: (0, 0))],
        out_specs=pl.BlockSpec((tm, d), lambda i: (i, 0)),
        out_shape=jax.ShapeDtypeStruct((n, d), out_dtype),
        compiler_params=_params(("parallel",)),
        name="rmsnorm",
    )(x, g)


def _gateup_kernel(h_ref, wg_ref, wu_ref, o_ref):
    h = h_ref[...]
    g = jnp.dot(h, wg_ref[...], preferred_element_type=f32)
    u = jnp.dot(h, wu_ref[...], preferred_element_type=f32)
    o_ref[...] = (g * jax.nn.sigmoid(g) * u).astype(o_ref.dtype)


def gateup(h, wg, wu, layer, tm=1024, tf=512):
    n, d = h.shape
    dff = wg.shape[-1]
    return pl.pallas_call(
        _gateup_kernel,
        grid=(n // tm, dff // tf),
        in_specs=[pl.BlockSpec((tm, d), lambda i, j: (i, 0)),
                  pl.BlockSpec((None, d, tf), lambda i, j: (layer, 0, j)),
                  pl.BlockSpec((None, d, tf), lambda i, j: (layer, 0, j))],
        out_specs=pl.BlockSpec((tm, tf), lambda i, j: (i, j)),
        out_shape=jax.ShapeDtypeStruct((n, dff), bf16),
        compiler_params=_params(("parallel", "arbitrary"), 48),
        name="ffn_gateup",
    )(h, wg, wu)


def _down_kernel(a_ref, w_ref, x_ref, g_ref, xo_ref, ho_ref):
    acc = jnp.dot(a_ref[...], w_ref[...], preferred_element_type=f32)
    xn = x_ref[...] + 0.5 * acc
    xo_ref[...] = xn
    ho_ref[...] = _rms(xn, g_ref[...]).astype(ho_ref.dtype)


def down_residual_norm(a, wd, layer, x, g_next, h_dtype, tm=256):
    n, dff = a.shape
    d = wd.shape[-1]
    return pl.pallas_call(
        _down_kernel,
        grid=(n // tm,),
        in_specs=[pl.BlockSpec((tm, dff), lambda i: (i, 0)),
                  pl.BlockSpec((None, dff, d), lambda i: (layer, 0, 0), pipeline_mode=pl.Buffered(1)),
                  pl.BlockSpec((tm, d), lambda i: (i, 0)),
                  pl.BlockSpec((1, d), lambda i: (0, 0))],
        out_specs=[pl.BlockSpec((tm, d), lambda i: (i, 0)),
                   pl.BlockSpec((tm, d), lambda i: (i, 0))],
        out_shape=[jax.ShapeDtypeStruct((n, d), f32),
                   jax.ShapeDtypeStruct((n, d), h_dtype)],
        compiler_params=_params(("parallel",), 56),
        name="ffn_down",
    )(a, wd, x, g_next)


def _inproj_kernel(h_ref, w_ref, o_ref):
    o_ref[...] = jnp.dot(h_ref[...], w_ref[...], preferred_element_type=f32).astype(o_ref.dtype)


def _ingate_kernel(h_ref, w_ref, b_ref, o_ref):
    acc = jnp.dot(h_ref[...], w_ref[...], preferred_element_type=f32)
    o_ref[...] = jax.nn.sigmoid(acc + b_ref[...]).astype(o_ref.dtype)


def in_projection(h, w_in, layer, tm=1024, tn=1024):
    n, d = h.shape
    return pl.pallas_call(
        _inproj_kernel,
        grid=(n // tm, Z_W // tn),
        in_specs=[pl.BlockSpec((tm, d), lambda i, j: (i, 0)),
                  pl.BlockSpec((None, d, tn), lambda i, j: (layer, 0, j))],
        out_specs=pl.BlockSpec((tm, tn), lambda i, j: (i, j)),
        out_shape=jax.ShapeDtypeStruct((n, Z_W), bf16),
        compiler_params=_params(("parallel", "arbitrary"), 48),
        name="mix_inproj",
    )(h, w_in)


def in_gates(h, w_in, b_gate, layer, tm=1024, tn=1024):
    n, d = h.shape
    off = Z_W // tn
    return pl.pallas_call(
        _ingate_kernel,
        grid=(n // tm, GATE_W // tn),
        in_specs=[pl.BlockSpec((tm, d), lambda i, j: (i, 0)),
                  pl.BlockSpec((None, d, tn), lambda i, j: (layer, 0, j + off)),
                  pl.BlockSpec((None, 1, tn), lambda i, j: (layer, 0, j))],
        out_specs=pl.BlockSpec((tm, tn), lambda i, j: (i, j)),
        out_shape=jax.ShapeDtypeStruct((n, GATE_W), bf16),
        compiler_params=_params(("parallel", "arbitrary"), 48),
        name="mix_ingate",
    )(h, w_in, b_gate)


def _retention_kernel(q_ref, k_ref, v_ref, g_ref, cos_ref, sin_ref, inner_ref, xi_ref, zeta_ref,
                      gch_ref, o_ref, state_ref, *, nchunk):
    @pl.when(pl.program_id(2) == 0)
    def _():
        state_ref[...] = jnp.zeros_like(state_ref)

    inner = inner_ref[...]
    xi = xi_ref[...]
    zeta = zeta_ref[...]
    gch = gch_ref[...]
    half = RET_QK_DIM // 2
    for c in range(nchunk):
        rows = pl.ds(c * BLOCK, BLOCK)
        cos = cos_ref[rows, :]
        sin = sin_ref[rows, :]

        def rot(t):
            t1, t2 = t[:, :half], t[:, half:]
            return jnp.concatenate([t1 * cos - t2 * sin, t1 * sin + t2 * cos], axis=-1)

        q = rot(q_ref[rows, :].astype(f32))
        k = rot(k_ref[rows, :].astype(f32)) * (RET_QK_DIM ** -0.5)
        v = v_ref[rows, :]
        qb = q.astype(bf16)
        a = lax.dot_general(qb, k.astype(bf16), (((1,), (1,)), ((), ())),
                            preferred_element_type=f32) * inner
        state = state_ref[...]
        o = (jnp.dot(a.astype(bf16), v, preferred_element_type=f32)
             + jnp.dot(qb, state.astype(bf16), preferred_element_type=f32) * xi)
        kzt = (k * zeta).T.astype(bf16)
        state_ref[...] = state * gch + jnp.dot(kzt, v, preferred_element_type=f32)
        on = o * lax.rsqrt(jnp.mean(o * o, axis=-1, keepdims=True) + EPS)
        gt = g_ref[rows, :].astype(f32)
        o_ref[rows, :] = (gt * jax.nn.sigmoid(gt) * on).astype(o_ref.dtype)


def _retention_tables(seq):
    h = jnp.arange(RET_HEADS, dtype=f32)
    log_g = jnp.log1p(-(2.0 ** (-5.0 - h)))
    idx = jnp.arange(BLOCK, dtype=f32)
    rel = idx[:, None] - idx[None, :]
    inner = jnp.where(rel[None] >= 0, jnp.exp(jnp.maximum(rel, 0.0)[None] * log_g[:, None, None]), 0.0)
    xi = jnp.exp((idx + 1.0)[None, :] * log_g[:, None])[..., None]
    zeta = jnp.exp((BLOCK - 1.0 - idx)[None, :] * log_g[:, None])[..., None]
    gch = jnp.broadcast_to(jnp.exp(BLOCK * log_g)[:, None, None], (RET_HEADS, 1, RET_V_DIM))
    inv = ROPE_BASE ** (-jnp.arange(0, RET_QK_DIM, 2, dtype=f32) / RET_QK_DIM)
    ang = jnp.arange(seq).astype(f32)[:, None] * inv[None, :]
    return jnp.cos(ang), jnp.sin(ang), inner, xi, zeta, gch


def retention(z, tables, tokens=512):
    b, s, _ = z.shape
    cos, sin, inner, xi, zeta, gch = tables
    nq = RET_QK_W // RET_QK_DIM
    nv = 2 * RET_QK_W // RET_V_DIM
    kern = functools.partial(_retention_kernel, nchunk=tokens // BLOCK)
    return pl.pallas_call(
        kern,
        grid=(b, RET_HEADS, s // tokens),
        in_specs=[pl.BlockSpec((None, tokens, RET_QK_DIM), lambda bi, h, t: (bi, t, h)),
                  pl.BlockSpec((None, tokens, RET_QK_DIM), lambda bi, h, t: (bi, t, nq + h)),
                  pl.BlockSpec((None, tokens, RET_V_DIM), lambda bi, h, t: (bi, t, nv + h)),
                  pl.BlockSpec((None, tokens, RET_V_DIM), lambda bi, h, t: (bi, t, nv + RET_HEADS + h)),
                  pl.BlockSpec((tokens, RET_QK_DIM // 2), lambda bi, h, t: (t, 0)),
                  pl.BlockSpec((tokens, RET_QK_DIM // 2), lambda bi, h, t: (t, 0)),
                  pl.BlockSpec((None, BLOCK, BLOCK), lambda bi, h, t: (h, 0, 0)),
                  pl.BlockSpec((None, BLOCK, 1), lambda bi, h, t: (h, 0, 0)),
                  pl.BlockSpec((None, BLOCK, 1), lambda bi, h, t: (h, 0, 0)),
                  pl.BlockSpec((None, 1, RET_V_DIM), lambda bi, h, t: (h, 0, 0))],
        out_specs=pl.BlockSpec((None, tokens, RET_V_DIM), lambda bi, h, t: (bi, t, h)),
        out_shape=jax.ShapeDtypeStruct((b, s, RET_V_W), bf16),
        scratch_shapes=[pltpu.VMEM((RET_QK_DIM, RET_V_DIM), f32)],
        compiler_params=_params(("parallel", "parallel", "arbitrary"), 32),
        name="mix_retention",
    )(z, z, z, z, cos, sin, inner, xi, zeta, gch)


def _sgu_kernel(u_ref, v_ref, lg_ref, lb_ref, w_ref, b_ref, o_ref, *, nchunk):
    u = jax.nn.gelu(u_ref[...].astype(f32))
    v = jax.nn.gelu(v_ref[...].astype(f32))
    mu = jnp.mean(v, axis=-1, keepdims=True)
    var = jnp.mean(jnp.square(v - mu), axis=-1, keepdims=True)
    vn = ((v - mu) * lax.rsqrt(var + EPS) * lg_ref[...] + lb_ref[...]).astype(bf16)
    row = lax.broadcasted_iota(jnp.int32, (BLOCK, BLOCK), 0)
    col = lax.broadcasted_iota(jnp.int32, (BLOCK, BLOCK), 1)
    causal = row >= col
    for g in range(SG_GROUPS):
        wc = jnp.where(causal, w_ref[g], 0.0).astype(bf16)
        bias = b_ref[g]
        cols = slice(g * SG_GROUP_DIM, (g + 1) * SG_GROUP_DIM)
        for c in range(nchunk):
            rows = slice(c * BLOCK, (c + 1) * BLOCK)
            mix = jnp.dot(wc, vn[rows, cols], preferred_element_type=f32) + bias
            o_ref[rows, cols] = (u[rows, cols] * mix).astype(o_ref.dtype)


def spatial_gating(z2, ln_g, ln_b, sg_w, sg_b, layer, tokens=512):
    n = z2.shape[0]
    off = (2 * RET_QK_W + 2 * RET_V_W) // SG_W
    kern = functools.partial(_sgu_kernel, nchunk=tokens // BLOCK)
    return pl.pallas_call(
        kern,
        grid=(n // tokens,),
        in_specs=[pl.BlockSpec((tokens, SG_W), lambda i: (i, off)),
                  pl.BlockSpec((tokens, SG_W), lambda i: (i, off + 1)),
                  pl.BlockSpec((None, 1, SG_W), lambda i: (layer, 0, 0)),
                  pl.BlockSpec((None, 1, SG_W), lambda i: (layer, 0, 0)),
                  pl.BlockSpec((None, SG_GROUPS, BLOCK, BLOCK), lambda i: (layer, 0, 0, 0)),
                  pl.BlockSpec((None, SG_GROUPS, BLOCK, 1), lambda i: (layer, 0, 0, 0))],
        out_specs=pl.BlockSpec((tokens, SG_W), lambda i: (i, 0)),
        out_shape=jax.ShapeDtypeStruct((n, SG_W), bf16),
        compiler_params=_params(("parallel",), 32),
        name="mix_sgu",
    )(z2, z2, ln_g, ln_b, sg_w, sg_b)


def _attention_kernel(q_ref, k_ref, v_ref, bias_ref, o_ref, kf_ref, vf_ref, qf_ref, acc_ref, m_ref, l_ref):
    sb = pl.program_id(2)

    @pl.when(sb == 0)
    def _():
        kf_ref[...] = k_ref[...].astype(f32)
        vf_ref[...] = v_ref[...].astype(f32)

    qf_ref[...] = q_ref[...].astype(f32)
    scale = ATT_HEAD_DIM ** -0.5

    def rows_of(start, d):
        return pl.ds(start, BLOCK) if d == 1 else pl.ds(start, BLOCK, stride=d)

    for bi, (window, d) in enumerate(DILATION_CONFIGS):
        assert window // d == BLOCK
        span = d * BLOCK
        blocks_per_super = ATT_SUPER // span
        shift = d.bit_length() - 1

        def body(it, carry, bi=bi, d=d, span=span, blocks_per_super=blocks_per_super, shift=shift):
            n_loc = lax.shift_right_logical(it, shift)
            r = lax.bitwise_and(it, d - 1)
            n_glob = sb * blocks_per_super + n_loc
            q_rows = rows_of(n_loc * span + r, d)
            cur = n_glob * span + r
            prev = jnp.maximum(n_glob - 1, 0) * span + r
            qb = qf_ref[q_rows, :].astype(bf16)
            kk = jnp.concatenate([kf_ref[rows_of(prev, d), :], kf_ref[rows_of(cur, d), :]], axis=0).astype(bf16)
            vv = jnp.concatenate([vf_ref[rows_of(prev, d), :], vf_ref[rows_of(cur, d), :]], axis=0).astype(bf16)
            first = (n_glob == 0).astype(jnp.int32)
            s = lax.dot_general(qb, kk, (((1,), (1,)), ((), ())), preferred_element_type=f32) * scale
            s = s + bias_ref[bi, first]
            mb = jnp.max(s, axis=-1, keepdims=True)
            p = jnp.exp(s - mb)
            lb = jnp.sum(p, axis=-1, keepdims=True)
            ob = jnp.dot(p.astype(bf16), vv, preferred_element_type=f32)
            mb = jnp.broadcast_to(mb, (BLOCK, ATT_HEAD_DIM))
            lb = jnp.broadcast_to(lb, (BLOCK, ATT_HEAD_DIM))
            if bi == 0:
                m_ref[q_rows, :] = mb
                l_ref[q_rows, :] = lb
                acc_ref[q_rows, :] = ob
            else:
                m_old = m_ref[q_rows, :]
                m_new = jnp.maximum(m_old, mb)
                wa = jnp.exp(m_old - m_new)
                wb = jnp.exp(mb - m_new)
                m_ref[q_rows, :] = m_new
                l_ref[q_rows, :] = wa * l_ref[q_rows, :] + wb * lb
                acc_ref[q_rows, :] = wa * acc_ref[q_rows, :] + wb * ob
            return carry

        lax.fori_loop(0, blocks_per_super * d, body, 0)

    o_ref[...] = (acc_ref[...] / l_ref[...]).astype(o_ref.dtype)


def _t5_bucket(dist):
    max_exact = REL_BUCKETS // 2
    d_f = jnp.maximum(dist, 1).astype(f32)
    large = max_exact + (jnp.log(d_f / max_exact) / math.log(REL_MAX_DIST / max_exact)
                         * (REL_BUCKETS - max_exact)).astype(jnp.int32)
    large = jnp.minimum(large, REL_BUCKETS - 1)
    return jnp.where(dist < max_exact, dist, large)


def _attention_bias(rel_bias):
    i = jnp.arange(BLOCK)[:, None]
    j = jnp.arange(2 * BLOCK)[None, :]
    steps = BLOCK + i - j
    band = (steps >= 0) & (steps <= BLOCK)
    tab = rel_bias.astype(f32)
    out = []
    for _, d in DILATION_CONFIGS:
        bias = jnp.transpose(tab[_t5_bucket(d * jnp.maximum(steps, 0))], (2, 0, 1))
        later = jnp.where(band[None], bias, NEG_INF)
        first = jnp.where((band & (j >= BLOCK))[None], bias, NEG_INF)
        out.append(jnp.stack([later, first], axis=1))
    return jnp.stack(out)


def dilated_attention(z, bias):
    b, s, _ = z.shape
    off = (2 * RET_QK_W + 2 * RET_V_W + 2 * SG_W) // ATT_HEAD_DIM
    return pl.pallas_call(
        _attention_kernel,
        grid=(b, ATT_HEADS, s // ATT_SUPER),
        in_specs=[pl.BlockSpec((None, ATT_SUPER, ATT_HEAD_DIM), lambda bi, h, t: (bi, t, off + h)),
                  pl.BlockSpec((None, s, ATT_HEAD_DIM), lambda bi, h, t: (bi, 0, off + ATT_HEADS + h)),
                  pl.BlockSpec((None, s, ATT_HEAD_DIM), lambda bi, h, t: (bi, 0, off + 2 * ATT_HEADS + h)),
                  pl.BlockSpec((N_BRANCH, None, 2, BLOCK, 2 * BLOCK), lambda bi, h, t: (0, h, 0, 0, 0))],
        out_specs=pl.BlockSpec((None, ATT_SUPER, ATT_HEAD_DIM), lambda bi, h, t: (bi, t, h)),
        out_shape=jax.ShapeDtypeStruct((b, s, ATT_W), bf16),
        scratch_shapes=[pltpu.VMEM((s, ATT_HEAD_DIM), f32),
                        pltpu.VMEM((s, ATT_HEAD_DIM), f32),
                        pltpu.VMEM((ATT_SUPER, ATT_HEAD_DIM), f32),
                        pltpu.VMEM((ATT_SUPER, ATT_HEAD_DIM), f32),
                        pltpu.VMEM((ATT_SUPER, ATT_HEAD_DIM), f32),
                        pltpu.VMEM((ATT_SUPER, ATT_HEAD_DIM), f32)],
        compiler_params=_params(("parallel", "parallel", "arbitrary"), 40),
        name="mix_attention",
    )(z, z, z, bias)


def _merge_kernel(ret_ref, sg_ref, att_ref, gate_ref, wr_ref, ws_ref, wa_ref, wo_ref, x_ref, g_ref,
                  xo_ref, ho_ref):
    d = D_MODEL
    merged = gate_ref[:, 0:d].astype(f32) * jnp.dot(ret_ref[...], wr_ref[...], preferred_element_type=f32)
    merged += gate_ref[:, d:2 * d].astype(f32) * jnp.dot(sg_ref[...], ws_ref[...], preferred_element_type=f32)
    merged += gate_ref[:, 2 * d:3 * d].astype(f32) * jnp.dot(att_ref[...], wa_ref[...], preferred_element_type=f32)
    xn = x_ref[...] + jnp.dot(merged.astype(bf16), wo_ref[...], preferred_element_type=f32)
    xo_ref[...] = xn
    ho_ref[...] = _rms(xn, g_ref[...]).astype(ho_ref.dtype)


def merge_project(ret, sg, att, gates, w_ret, w_sg, w_att, w_out, layer, x, g_next, tm=256):
    n, d = x.shape

    def resident(w):
        return pl.BlockSpec((None,) + w.shape[1:], lambda i: (layer, 0, 0), pipeline_mode=pl.Buffered(1))

    def rows(w):
        return pl.BlockSpec((tm, w), lambda i: (i, 0))

    return pl.pallas_call(
        _merge_kernel,
        grid=(n // tm,),
        in_specs=[rows(RET_V_W), rows(SG_W), rows(ATT_W), rows(GATE_W),
                  resident(w_ret), resident(w_sg), resident(w_att), resident(w_out),
                  rows(d), pl.BlockSpec((1, d), lambda i: (0, 0))],
        out_specs=[rows(d), rows(d)],
        out_shape=[jax.ShapeDtypeStruct((n, d), f32), jax.ShapeDtypeStruct((n, d), bf16)],
        compiler_params=_params(("parallel",), 60),
        name="mix_merge",
    )(ret, sg, att, gates, w_ret, w_sg, w_att, w_out, x, g_next)


def kernel(x, ffn1_norm, ffn1_w_gate, ffn1_w_up, ffn1_w_down, mix_norm, w_in, b_gate, sg_ln_g, sg_ln_b,
           sg_w, sg_b, rel_bias, w_proj_ret, w_proj_sg, w_proj_att, w_out, ffn2_norm, ffn2_w_gate,
           ffn2_w_up, ffn2_w_down, final_norm):
    b, s, d = x.shape
    n = b * s
    depth = ffn1_norm.shape[0]

    def w16(w):
        return w.astype(bf16)

    ffn1_wg, ffn1_wu, ffn1_wd = w16(ffn1_w_gate), w16(ffn1_w_up), w16(ffn1_w_down)
    ffn2_wg, ffn2_wu, ffn2_wd = w16(ffn2_w_gate), w16(ffn2_w_up), w16(ffn2_w_down)
    w_in16, w_ret16, w_sg16, w_att16, w_out16 = (w16(w_in), w16(w_proj_ret), w16(w_proj_sg),
                                                 w16(w_proj_att), w16(w_out))
    b_gate3 = b_gate[:, None, :]
    ln_g3, ln_b3 = sg_ln_g[:, None, :], sg_ln_b[:, None, :]
    sg_b4 = sg_b[..., None]
    tables = _retention_tables(s)
    att_bias = _attention_bias(rel_bias)

    xr = x.reshape(n, d)
    h = rmsnorm(xr, ffn1_norm[0][None], bf16)
    for l in range(depth):
        a = gateup(h, ffn1_wg, ffn1_wu, l)
        xr, h = down_residual_norm(a, ffn1_wd, l, xr, mix_norm[l][None], bf16)

        z = in_projection(h, w_in16, l)
        gates = in_gates(h, w_in16, b_gate3, l)
        z3 = z.reshape(b, s, Z_W)
        ret = retention(z3, tables).reshape(n, RET_V_W)
        sg = spatial_gating(z, ln_g3, ln_b3, sg_w, sg_b4, l)
        att = dilated_attention(z3, att_bias).reshape(n, ATT_W)
        xr, h = merge_project(ret, sg, att, gates, w_ret16, w_sg16, w_att16, w_out16, l, xr,
                              ffn2_norm[l][None])

        a = gateup(h, ffn2_wg, ffn2_wu, l)
        last = l == depth - 1
        g_next = final_norm[None] if last else ffn1_norm[l + 1][None]
        xr, h = down_residual_norm(a, ffn2_wd, l, xr, g_next, f32 if last else bf16)
    return h.reshape(b, s, d)
```

```python
import functools
import math

import jax
import jax.numpy as jnp
from jax import lax
from jax.experimental import pallas as pl
from jax.experimental.pallas import tpu as pltpu

f32 = jnp.float32
bf16 = jnp.bfloat16

D_MODEL = 2048
DEPTH = 4
BLOCK = 128
RET_HEADS = 4
RET_QK_DIM = 256
RET_V_DIM = 512
SG_GROUPS = 4
SG_GROUP_DIM = 256
ATT_HEADS = 8
ATT_HEAD_DIM = 128
DILATION_CONFIGS = ((128, 1), (512, 4), (2048, 16))
REL_BUCKETS = 32
REL_MAX_DIST = 2048
D_FF = 5632
ROPE_BASE = 10000.0
EPS = 1e-6
NEG_INF = -1e30

RET_QK_W = RET_HEADS * RET_QK_DIM
RET_V_W = RET_HEADS * RET_V_DIM
SG_W = SG_GROUPS * SG_GROUP_DIM
ATT_W = ATT_HEADS * ATT_HEAD_DIM
N_BRANCH = 3
Z_W = 2 * RET_QK_W + 2 * RET_V_W + 2 * SG_W + 3 * ATT_W
GATE_W = N_BRANCH * D_MODEL

ATT_SUPER = 2048
ATT_GROUP = 4
ATT_PAD_EVERY = 4
V7X_VMEM_LIMIT = 60 * 1024 * 1024


def _params(sem, vmem_mb=None):
    return pltpu.CompilerParams(
        dimension_semantics=sem,
        vmem_limit_bytes=None if vmem_mb is None else min(vmem_mb * 1024 * 1024, V7X_VMEM_LIMIT))


def _rms(x, g):
    return x * lax.rsqrt(jnp.mean(x * x, axis=-1, keepdims=True) + EPS) * g


def _rmsnorm_kernel(x_ref, g_ref, o_ref):
    o_ref[...] = _rms(x_ref[...], g_ref[...]).astype(o_ref.dtype)


def rmsnorm(x, g, out_dtype, tm=512):
    n, d = x.shape
    return pl.pallas_call(
        _rmsnorm_kernel,
        grid=(n // tm,),
        in_specs=[pl.BlockSpec((tm, d), lambda i: (i, 0)),
                  pl.BlockSpec((1, d), lambda i: (0, 0))],
        out_specs=pl.BlockSpec((tm, d), lambda i: (i, 0)),
        out_shape=jax.ShapeDtypeStruct((n, d), out_dtype),
        compiler_params=_params(("parallel",)),
        name="rmsnorm",
    )(x, g)


def _gateup_kernel(h_ref, wg_ref, wu_ref, o_ref):
    h = h_ref[...]
    g = jnp.dot(h, wg_ref[...], preferred_element_type=f32)
    u = jnp.dot(h, wu_ref[...], preferred_element_type=f32)
    o_ref[...] = (g * jax.nn.sigmoid(g) * u).astype(o_ref.dtype)


def gateup(h, wg, wu, layer, tm=1024, tf=512):
    n, d = h.shape
    dff = wg.shape[-1]
    return pl.pallas_call(
        _gateup_kernel,
        grid=(n // tm, dff // tf),
        in_specs=[pl.BlockSpec((tm, d), lambda i, j: (i, 0)),
                  pl.BlockSpec((None, d, tf), lambda i, j: (layer, 0, j)),
                  pl.BlockSpec((None, d, tf), lambda i, j: (layer, 0, j))],
        out_specs=pl.BlockSpec((tm, tf), lambda i, j: (i, j)),
        out_shape=jax.ShapeDtypeStruct((n, dff), bf16),
        compiler_params=_params(("parallel", "arbitrary"), 48),
        name="ffn_gateup",
    )(h, wg, wu)


def _down_kernel(a_ref, w_ref, x_ref, g_ref, xo_ref, ho_ref):
    acc = jnp.dot(a_ref[...], w_ref[...], preferred_element_type=f32)
    xn = x_ref[...] + 0.5 * acc
    xo_ref[...] = xn
    ho_ref[...] = _rms(xn, g_ref[...]).astype(ho_ref.dtype)


def down_residual_norm(a, wd, layer, x, g_next, h_dtype, tm=256):
    n, dff = a.shape
    d = wd.shape[-1]
    return pl.pallas_call(
        _down_kernel,
        grid=(n // tm,),
        in_specs=[pl.BlockSpec((tm, dff), lambda i: (i, 0)),
                  pl.BlockSpec((None, dff, d), lambda i: (layer, 0, 0), pipeline_mode=pl.Buffered(1)),
                  pl.BlockSpec((tm, d), lambda i: (i, 0)),
                  pl.BlockSpec((1, d), lambda i: (0, 0))],
        out_specs=[pl.BlockSpec((tm, d), lambda i: (i, 0)),
                   pl.BlockSpec((tm, d), lambda i: (i, 0))],
        out_shape=[jax.ShapeDtypeStruct((n, d), f32),
                   jax.ShapeDtypeStruct((n, d), h_dtype)],
        compiler_params=_params(("parallel",), 56),
        name="ffn_down",
    )(a, wd, x, g_next)


def _inproj_kernel(h_ref, w_ref, o_ref):
    o_ref[...] = jnp.dot(h_ref[...], w_ref[...], preferred_element_type=f32).astype(o_ref.dtype)


def _ingate_kernel(h_ref, w_ref, b_ref, o_ref):
    acc = jnp.dot(h_ref[...], w_ref[...], preferred_element_type=f32)
    o_ref[...] = jax.nn.sigmoid(acc + b_ref[...]).astype(o_ref.dtype)


def in_projection(h, w_in, layer, tm=1024, tn=1024):
    n, d = h.shape
    return pl.pallas_call(
        _inproj_kernel,
        grid=(n // tm, Z_W // tn),
        in_specs=[pl.BlockSpec((tm, d), lambda i, j: (i, 0)),
                  pl.BlockSpec((None, d, tn), lambda i, j: (layer, 0, j))],
        out_specs=pl.BlockSpec((tm, tn), lambda i, j: (i, j)),
        out_shape=jax.ShapeDtypeStruct((n, Z_W), bf16),
        compiler_params=_params(("parallel", "arbitrary"), 48),
        name="mix_inproj",
    )(h, w_in)


def in_gates(h, w_in, b_gate, layer, tm=1024, tn=1024):
    n, d = h.shape
    off = Z_W // tn
    return pl.pallas_call(
        _ingate_kernel,
        grid=(n // tm, GATE_W // tn),
        in_specs=[pl.BlockSpec((tm, d), lambda i, j: (i, 0)),
                  pl.BlockSpec((None, d, tn), lambda i, j: (layer, 0, j + off)),
                  pl.BlockSpec((None, 1, tn), lambda i, j: (layer, 0, j))],
        out_specs=pl.BlockSpec((tm, tn), lambda i, j: (i, j)),
        out_shape=jax.ShapeDtypeStruct((n, GATE_W), bf16),
        compiler_params=_params(("parallel", "arbitrary"), 48),
        name="mix_ingate",
    )(h, w_in, b_gate)


def _retention_kernel(q_ref, k_ref, v_ref, g_ref, cos_ref, sin_ref, inner_ref, xi_ref, zeta_ref,
                      gch_ref, o_ref, state_ref, *, nchunk):
    @pl.when(pl.program_id(2) == 0)
    def _():
        state_ref[...] = jnp.zeros_like(state_ref)

    inner = inner_ref[...]
    xi = xi_ref[...]
    zeta = zeta_ref[...]
    gch = gch_ref[...]
    half = RET_QK_DIM // 2
    for c in range(nchunk):
        rows = pl.ds(c * BLOCK, BLOCK)
        cos = cos_ref[rows, :]
        sin = sin_ref[rows, :]

        def rot(t):
            t1, t2 = t[:, :half], t[:, half:]
            return jnp.concatenate([t1 * cos - t2 * sin, t1 * sin + t2 * cos], axis=-1)

        q = rot(q_ref[rows, :].astype(f32))
        k = rot(k_ref[rows, :].astype(f32)) * (RET_QK_DIM ** -0.5)
        v = v_ref[rows, :]
        qb = q.astype(bf16)
        a = lax.dot_general(qb, k.astype(bf16), (((1,), (1,)), ((), ())),
                            preferred_element_type=f32) * inner
        state = state_ref[...]
        o = (jnp.dot(a.astype(bf16), v, preferred_element_type=f32)
             + jnp.dot(qb, state.astype(bf16), preferred_element_type=f32) * xi)
        kzt = (k * zeta).T.astype(bf16)
        state_ref[...] = state * gch + jnp.dot(kzt, v, preferred_element_type=f32)
        on = o * lax.rsqrt(jnp.mean(o * o, axis=-1, keepdims=True) + EPS)
        gt = g_ref[rows, :].astype(f32)
        o_ref[rows, :] = (gt * jax.nn.sigmoid(gt) * on).astype(o_ref.dtype)


def _retention_tables(seq):
    h = jnp.arange(RET_HEADS, dtype=f32)
    log_g = jnp.log1p(-(2.0 ** (-5.0 - h)))
    idx = jnp.arange(BLOCK, dtype=f32)
    rel = idx[:, None] - idx[None, :]
    inner = jnp.where(rel[None] >= 0, jnp.exp(jnp.maximum(rel, 0.0)[None] * log_g[:, None, None]), 0.0)
    xi = jnp.exp((idx + 1.0)[None, :] * log_g[:, None])[..., None]
    zeta = jnp.exp((BLOCK - 1.0 - idx)[None, :] * log_g[:, None])[..., None]
    gch = jnp.broadcast_to(jnp.exp(BLOCK * log_g)[:, None, None], (RET_HEADS, 1, RET_V_DIM))
    inv = ROPE_BASE ** (-jnp.arange(0, RET_QK_DIM, 2, dtype=f32) / RET_QK_DIM)
    ang = jnp.arange(seq).astype(f32)[:, None] * inv[None, :]
    return jnp.cos(ang), jnp.sin(ang), inner, xi, zeta, gch


def retention(z, tables, tokens=512):
    b, s, _ = z.shape
    cos, sin, inner, xi, zeta, gch = tables
    nq = RET_QK_W // RET_QK_DIM
    nv = 2 * RET_QK_W // RET_V_DIM
    kern = functools.partial(_retention_kernel, nchunk=tokens // BLOCK)
    return pl.pallas_call(
        kern,
        grid=(b, RET_HEADS, s // tokens),
        in_specs=[pl.BlockSpec((None, tokens, RET_QK_DIM), lambda bi, h, t: (bi, t, h)),
                  pl.BlockSpec((None, tokens, RET_QK_DIM), lambda bi, h, t: (bi, t, nq + h)),
                  pl.BlockSpec((None, tokens, RET_V_DIM), lambda bi, h, t: (bi, t, nv + h)),
                  pl.BlockSpec((None, tokens, RET_V_DIM), lambda bi, h, t: (bi, t, nv + RET_HEADS + h)),
                  pl.BlockSpec((tokens, RET_QK_DIM // 2), lambda bi, h, t: (t, 0)),
                  pl.BlockSpec((tokens, RET_QK_DIM // 2), lambda bi, h, t: (t, 0)),
                  pl.BlockSpec((None, BLOCK, BLOCK), lambda bi, h, t: (h, 0, 0)),
                  pl.BlockSpec((None, BLOCK, 1), lambda bi, h, t: (h, 0, 0)),
                  pl.BlockSpec((None, BLOCK, 1), lambda bi, h, t: (h, 0, 0)),
                  pl.BlockSpec((None, 1, RET_V_DIM), lambda bi, h, t: (h, 0, 0))],
        out_specs=pl.BlockSpec((None, tokens, RET_V_DIM), lambda bi, h, t: (bi, t, h)),
        out_shape=jax.ShapeDtypeStruct((b, s, RET_V_W), bf16),
        scratch_shapes=[pltpu.VMEM((RET_QK_DIM, RET_V_DIM), f32)],
        compiler_params=_params(("parallel", "parallel", "arbitrary"), 32),
        name="mix_retention",
    )(z, z, z, z, cos, sin, inner, xi, zeta, gch)


def _sgu_kernel(u_ref, v_ref, lg_ref, lb_ref, w_ref, b_ref, o_ref, *, nchunk):
    u = jax.nn.gelu(u_ref[...].astype(f32))
    v = jax.nn.gelu(v_ref[...].astype(f32))
    mu = jnp.mean(v, axis=-1, keepdims=True)
    var = jnp.mean(jnp.square(v - mu), axis=-1, keepdims=True)
    vn = ((v - mu) * lax.rsqrt(var + EPS) * lg_ref[...] + lb_ref[...]).astype(bf16)
    row = lax.broadcasted_iota(jnp.int32, (BLOCK, BLOCK), 0)
    col = lax.broadcasted_iota(jnp.int32, (BLOCK, BLOCK), 1)
    causal = row >= col
    for g in range(SG_GROUPS):
        wc = jnp.where(causal, w_ref[g], 0.0).astype(bf16)
        bias = b_ref[g]
        cols = slice(g * SG_GROUP_DIM, (g + 1) * SG_GROUP_DIM)
        for c in range(nchunk):
            rows = slice(c * BLOCK, (c + 1) * BLOCK)
            mix = jnp.dot(wc, vn[rows, cols], preferred_element_type=f32) + bias
            o_ref[rows, cols] = (u[rows, cols] * mix).astype(o_ref.dtype)


def spatial_gating(z2, ln_g, ln_b, sg_w, sg_b, layer, tokens=512):
    n = z2.shape[0]
    off = (2 * RET_QK_W + 2 * RET_V_W) // SG_W
    kern = functools.partial(_sgu_kernel, nchunk=tokens // BLOCK)
    return pl.pallas_call(
        kern,
        grid=(n // tokens,),
        in_specs=[pl.BlockSpec((tokens, SG_W), lambda i: (i, off)),
                  pl.BlockSpec((tokens, SG_W), lambda i: (i, off + 1)),
                  pl.BlockSpec((None, 1, SG_W), lambda i: (layer, 0, 0)),
                  pl.BlockSpec((None, 1, SG_W), lambda i: (layer, 0, 0)),
                  pl.BlockSpec((None, SG_GROUPS, BLOCK, BLOCK), lambda i: (layer, 0, 0, 0)),
                  pl.BlockSpec((None, SG_GROUPS, BLOCK, 1), lambda i: (layer, 0, 0, 0))],
        out_specs=pl.BlockSpec((tokens, SG_W), lambda i: (i, 0)),
        out_shape=jax.ShapeDtypeStruct((n, SG_W), bf16),
        compiler_params=_params(("parallel",), 32),
        name="mix_sgu",
    )(z2, z2, ln_g, ln_b, sg_w, sg_b)


def _padded(t):
    return t + t // ATT_PAD_EVERY


def _attention_kernel(q_ref, k_ref, v_ref, bias_ref, o_ref, kp_ref, vp_ref, qp_ref, stage_ref,
                      mp_ref, lp_ref, accp_ref, mn_ref, ln_ref, accn_ref):
    sb = pl.program_id(2)
    seq = k_ref.shape[0]
    scale = ATT_HEAD_DIM ** -0.5
    quarter = ATT_SUPER // ATT_PAD_EVERY
    super_p = _padded(ATT_SUPER)

    def pad_rows(dst_ref, dst_base):
        for j in range(ATT_PAD_EVERY):
            dst_ref[pl.ds(dst_base + j, quarter, stride=ATT_PAD_EVERY + 1), :] = (
                stage_ref[pl.ds(j, quarter, stride=ATT_PAD_EVERY), :])

    @pl.when(sb == 0)
    def _():
        for src_ref, dst_ref in ((k_ref, kp_ref), (v_ref, vp_ref)):
            for c in range(seq // ATT_SUPER):
                stage_ref[...] = src_ref[pl.ds(c * ATT_SUPER, ATT_SUPER), :].astype(f32)
                pad_rows(dst_ref, c * super_p)

    stage_ref[...] = q_ref[...].astype(f32)
    pad_rows(qp_ref, 0)

    def softmax(s):
        mb = jnp.max(s, axis=-1, keepdims=True)
        p = jnp.exp(s - mb)
        return mb, jnp.sum(p, axis=-1, keepdims=True), p.astype(bf16)

    def merged(state, mb, lb, ob):
        mb = jnp.broadcast_to(mb, (BLOCK, ATT_HEAD_DIM))
        lb = jnp.broadcast_to(lb, (BLOCK, ATT_HEAD_DIM))
        if state is None:
            return mb, lb, ob
        m_old, l_old, acc_old = state
        m_new = jnp.maximum(m_old, mb)
        wa = jnp.exp(m_old - m_new)
        wb = jnp.exp(mb - m_new)
        return m_new, wa * l_old + wb * lb, wa * acc_old + wb * ob

    def run_pass(nblocks, load, finish):
        def body(step, carry):
            its = [step * ATT_GROUP + g for g in range(ATT_GROUP)]
            ins = [load(it) for it in its]
            logit = [lax.dot_general(q, k, (((1,), (1,)), ((), ())), preferred_element_type=f32) * scale + b
                     for q, k, _, b, _ in ins]
            soft = [softmax(s) for s in logit]
            outs = [jnp.dot(soft[g][2], ins[g][2], preferred_element_type=f32) for g in range(ATT_GROUP)]
            new = [merged(ins[g][4], soft[g][0], soft[g][1], outs[g]) for g in range(ATT_GROUP)]
            for g in range(ATT_GROUP):
                finish(its[g], *new[g])
            return carry

        lax.fori_loop(0, nblocks // ATT_GROUP, body, 0)

    for bi, (window, d) in reversed(list(enumerate(DILATION_CONFIGS))[1:]):
        assert window // d == BLOCK and ATT_SUPER % (d * BLOCK) == 0 and d % ATT_PAD_EVERY == 0
        span_p = _padded(d * BLOCK)
        blocks_per_super = ATT_SUPER // (d * BLOCK)
        shift = d.bit_length() - 1
        stride = _padded(d)
        is_first_pass = bi == len(DILATION_CONFIGS) - 1

        def rows(start, stride=stride):
            return pl.ds(start, BLOCK, stride=stride)

        def load(it, bi=bi, d=d, span_p=span_p, blocks_per_super=blocks_per_super, shift=shift,
                 rows=rows, is_first_pass=is_first_pass):
            n_loc = lax.shift_right_logical(it, shift)
            r = lax.bitwise_and(it, d - 1)
            r_p = r + lax.shift_right_logical(r, 2)
            n_glob = sb * blocks_per_super + n_loc
            q_rows = rows(n_loc * span_p + r_p)
            cur = n_glob * span_p + r_p
            prev = jnp.maximum(n_glob - 1, 0) * span_p + r_p
            k = jnp.concatenate([kp_ref[rows(prev), :], kp_ref[rows(cur), :]], axis=0).astype(bf16)
            v = jnp.concatenate([vp_ref[rows(prev), :], vp_ref[rows(cur), :]], axis=0).astype(bf16)
            first = (n_glob == 0).astype(jnp.int32)
            state = None if is_first_pass else (mp_ref[q_rows, :], lp_ref[q_rows, :], accp_ref[q_rows, :])
            return qp_ref[q_rows, :].astype(bf16), k, v, bias_ref[bi, first], state

        def finish(it, m, l, acc, d=d, span_p=span_p, shift=shift, rows=rows):
            n_loc = lax.shift_right_logical(it, shift)
            r = lax.bitwise_and(it, d - 1)
            q_rows = rows(n_loc * span_p + r + lax.shift_right_logical(r, 2))
            mp_ref[q_rows, :], lp_ref[q_rows, :], accp_ref[q_rows, :] = m, l, acc

        run_pass(blocks_per_super * d, load, finish)

    for src_ref, dst_ref in ((mp_ref, mn_ref), (lp_ref, ln_ref), (accp_ref, accn_ref)):
        for j in range(ATT_PAD_EVERY):
            dst_ref[pl.ds(j, quarter, stride=ATT_PAD_EVERY), :] = (
                src_ref[pl.ds(j, quarter, stride=ATT_PAD_EVERY + 1), :])

    window, d = DILATION_CONFIGS[0]
    assert d == 1 and window == BLOCK

    def load_dense(it):
        n_glob = sb * (ATT_SUPER // BLOCK) + it
        q_rows = pl.ds(pl.multiple_of(it * BLOCK, BLOCK), BLOCK)
        cur = pl.ds(pl.multiple_of(n_glob * BLOCK, BLOCK), BLOCK)
        prev = pl.ds(pl.multiple_of(jnp.maximum(n_glob - 1, 0) * BLOCK, BLOCK), BLOCK)
        k = jnp.concatenate([k_ref[prev, :], k_ref[cur, :]], axis=0)
        v = jnp.concatenate([v_ref[prev, :], v_ref[cur, :]], axis=0)
        first = (n_glob == 0).astype(jnp.int32)
        return q_ref[q_rows, :], k, v, bias_ref[0, first], (mn_ref[q_rows, :], ln_ref[q_rows, :], accn_ref[q_rows, :])

    def finish_dense(it, m, l, acc):
        o_ref[pl.ds(pl.multiple_of(it * BLOCK, BLOCK), BLOCK), :] = (acc / l).astype(o_ref.dtype)

    run_pass(ATT_SUPER // BLOCK, load_dense, finish_dense)


def _t5_bucket(dist):
    max_exact = REL_BUCKETS // 2
    d_f = jnp.maximum(dist, 1).astype(f32)
    large = max_exact + (jnp.log(d_f / max_exact) / math.log(REL_MAX_DIST / max_exact)
                         * (REL_BUCKETS - max_exact)).astype(jnp.int32)
    large = jnp.minimum(large, REL_BUCKETS - 1)
    return jnp.where(dist < max_exact, dist, large)


def _attention_bias(rel_bias):
    i = jnp.arange(BLOCK)[:, None]
    j = jnp.arange(2 * BLOCK)[None, :]
    steps = BLOCK + i - j
    band = (steps >= 0) & (steps <= BLOCK)
    tab = rel_bias.astype(f32)
    out = []
    for _, d in DILATION_CONFIGS:
        bias = jnp.transpose(tab[_t5_bucket(d * jnp.maximum(steps, 0))], (2, 0, 1))
        later = jnp.where(band[None], bias, NEG_INF)
        first = jnp.where((band & (j >= BLOCK))[None], bias, NEG_INF)
        out.append(jnp.stack([later, first], axis=1))
    return jnp.stack(out)


def dilated_attention(z, bias):
    b, s, _ = z.shape
    off = (2 * RET_QK_W + 2 * RET_V_W + 2 * SG_W) // ATT_HEAD_DIM
    return pl.pallas_call(
        _attention_kernel,
        grid=(b, ATT_HEADS, s // ATT_SUPER),
        in_specs=[pl.BlockSpec((None, ATT_SUPER, ATT_HEAD_DIM), lambda bi, h, t: (bi, t, off + h)),
                  pl.BlockSpec((None, s, ATT_HEAD_DIM), lambda bi, h, t: (bi, 0, off + ATT_HEADS + h)),
                  pl.BlockSpec((None, s, ATT_HEAD_DIM), lambda bi, h, t: (bi, 0, off + 2 * ATT_HEADS + h)),
                  pl.BlockSpec((N_BRANCH, None, 2, BLOCK, 2 * BLOCK), lambda bi, h, t: (0, h, 0, 0, 0))],
        out_specs=pl.BlockSpec((None, ATT_SUPER, ATT_HEAD_DIM), lambda bi, h, t: (bi, t, h)),
        out_shape=jax.ShapeDtypeStruct((b, s, ATT_W), bf16),
        scratch_shapes=([pltpu.VMEM((_padded(s), ATT_HEAD_DIM), f32)] * 2
                        + [pltpu.VMEM((_padded(ATT_SUPER), ATT_HEAD_DIM), f32)]
                        + [pltpu.VMEM((ATT_SUPER, ATT_HEAD_DIM), f32)]
                        + [pltpu.VMEM((_padded(ATT_SUPER), ATT_HEAD_DIM), f32)] * 3
                        + [pltpu.VMEM((ATT_SUPER, ATT_HEAD_DIM), f32)] * 3),
        compiler_params=_params(("parallel", "parallel", "arbitrary"), 40),
        name="mix_attention",
    )(z, z, z, bias)


def _merge_kernel(ret_ref, sg_ref, att_ref, gate_ref, wr_ref, ws_ref, wa_ref, wo_ref, x_ref, g_ref,
                  xo_ref, ho_ref):
    d = D_MODEL
    merged = gate_ref[:, 0:d].astype(f32) * jnp.dot(ret_ref[...], wr_ref[...], preferred_element_type=f32)
    merged += gate_ref[:, d:2 * d].astype(f32) * jnp.dot(sg_ref[...], ws_ref[...], preferred_element_type=f32)
    merged += gate_ref[:, 2 * d:3 * d].astype(f32) * jnp.dot(att_ref[...], wa_ref[...], preferred_element_type=f32)
    xn = x_ref[...] + jnp.dot(merged.astype(bf16), wo_ref[...], preferred_element_type=f32)
    xo_ref[...] = xn
    ho_ref[...] = _rms(xn, g_ref[...]).astype(ho_ref.dtype)


def merge_project(ret, sg, att, gates, w_ret, w_sg, w_att, w_out, layer, x, g_next, tm=256):
    n, d = x.shape

    def resident(w):
        return pl.BlockSpec((None,) + w.shape[1:], lambda i: (layer, 0, 0), pipeline_mode=pl.Buffered(1))

    def rows(w):
        return pl.BlockSpec((tm, w), lambda i: (i, 0))

    return pl.pallas_call(
        _merge_kernel,
        grid=(n // tm,),
        in_specs=[rows(RET_V_W), rows(SG_W), rows(ATT_W), rows(GATE_W),
                  resident(w_ret), resident(w_sg), resident(w_att), resident(w_out),
                  rows(d), pl.BlockSpec((1, d), lambda i: (0, 0))],
        out_specs=[rows(d), rows(d)],
        out_shape=[jax.ShapeDtypeStruct((n, d), f32), jax.ShapeDtypeStruct((n, d), bf16)],
        compiler_params=_params(("parallel",), 60),
        name="mix_merge",
    )(ret, sg, att, gates, w_ret, w_sg, w_att, w_out, x, g_next)


def kernel(x, ffn1_norm, ffn1_w_gate, ffn1_w_up, ffn1_w_down, mix_norm, w_in, b_gate, sg_ln_g, sg_ln_b,
           sg_w, sg_b, rel_bias, w_proj_ret, w_proj_sg, w_proj_att, w_out, ffn2_norm, ffn2_w_gate,
           ffn2_w_up, ffn2_w_down, final_norm):
    b, s, d = x.shape
    n = b * s
    depth = ffn1_norm.shape[0]

    def w16(w):
        return w.astype(bf16)

    ffn1_wg, ffn1_wu, ffn1_wd = w16(ffn1_w_gate), w16(ffn1_w_up), w16(ffn1_w_down)
    ffn2_wg, ffn2_wu, ffn2_wd = w16(ffn2_w_gate), w16(ffn2_w_up), w16(ffn2_w_down)
    w_in16, w_ret16, w_sg16, w_att16, w_out16 = (w16(w_in), w16(w_proj_ret), w16(w_proj_sg),
                                                 w16(w_proj_att), w16(w_out))
    b_gate3 = b_gate[:, None, :]
    ln_g3, ln_b3 = sg_ln_g[:, None, :], sg_ln_b[:, None, :]
    sg_b4 = sg_b[..., None]
    tables = _retention_tables(s)
    att_bias = _attention_bias(rel_bias)

    xr = x.reshape(n, d)
    h = rmsnorm(xr, ffn1_norm[0][None], bf16)
    for l in range(depth):
        a = gateup(h, ffn1_wg, ffn1_wu, l)
        xr, h = down_residual_norm(a, ffn1_wd, l, xr, mix_norm[l][None], bf16)

        z = in_projection(h, w_in16, l)
        gates = in_gates(h, w_in16, b_gate3, l)
        z3 = z.reshape(b, s, Z_W)
        ret = retention(z3, tables).reshape(n, RET_V_W)
        sg = spatial_gating(z, ln_g3, ln_b3, sg_w, sg_b4, l)
        att = dilated_attention(z3, att_bias).reshape(n, ATT_W)
        xr, h = merge_project(ret, sg, att, gates, w_ret16, w_sg16, w_att16, w_out16, l, xr,
                              ffn2_norm[l][None])

        a = gateup(h, ffn2_wg, ffn2_wu, l)
        last = l == depth - 1
        g_next = final_norm[None] if last else ffn1_norm[l + 1][None]
        xr, h = down_residual_norm(a, ffn2_wd, l, xr, g_next, f32 if last else bf16)
    return h.reshape(b, s, d)
```

```python
import functools
import math

import jax
import jax.numpy as jnp
from jax import lax
from jax.experimental import pallas as pl
from jax.experimental.pallas import tpu as pltpu

f32 = jnp.float32
bf16 = jnp.bfloat16

D_MODEL = 2048
DEPTH = 4
BLOCK = 128
RET_HEADS = 4
RET_QK_DIM = 256
RET_V_DIM = 512
SG_GROUPS = 4
SG_GROUP_DIM = 256
ATT_HEADS = 8
ATT_HEAD_DIM = 128
DILATION_CONFIGS = ((128, 1), (512, 4), (2048, 16))
REL_BUCKETS = 32
REL_MAX_DIST = 2048
D_FF = 5632
ROPE_BASE = 10000.0
EPS = 1e-6
NEG_INF = -1e30

RET_QK_W = RET_HEADS * RET_QK_DIM
RET_V_W = RET_HEADS * RET_V_DIM
SG_W = SG_GROUPS * SG_GROUP_DIM
ATT_W = ATT_HEADS * ATT_HEAD_DIM
N_BRANCH = 3
Z_W = 2 * RET_QK_W + 2 * RET_V_W + 2 * SG_W + 3 * ATT_W
GATE_W = N_BRANCH * D_MODEL

ATT_SUPER = 2048
ATT_GROUP = 4
ATT_GROUP_FIRST = 8
ATT_PAD_EVERY = 4
MM_ROW_CHUNK = 256
V7X_VMEM_LIMIT = 60 * 1024 * 1024


def _params(sem, vmem_mb=None):
    return pltpu.CompilerParams(
        dimension_semantics=sem,
        vmem_limit_bytes=None if vmem_mb is None else min(vmem_mb * 1024 * 1024, V7X_VMEM_LIMIT))


def _rms(x, g):
    return x * lax.rsqrt(jnp.mean(x * x, axis=-1, keepdims=True) + EPS) * g


def _rmsnorm_kernel(x_ref, g_ref, o_ref):
    o_ref[...] = _rms(x_ref[...], g_ref[...]).astype(o_ref.dtype)


def rmsnorm(x, g, out_dtype, tm=512):
    n, d = x.shape
    return pl.pallas_call(
        _rmsnorm_kernel,
        grid=(n // tm,),
        in_specs=[pl.BlockSpec((tm, d), lambda i: (i, 0)),
                  pl.BlockSpec((1, d), lambda i: (0, 0))],
        out_specs=pl.BlockSpec((tm, d), lambda i: (i, 0)),
        out_shape=jax.ShapeDtypeStruct((n, d), out_dtype),
        compiler_params=_params(("parallel",)),
        name="rmsnorm",
    )(x, g)


def _row_chunks(rows):
    return [pl.ds(c * MM_ROW_CHUNK, MM_ROW_CHUNK) for c in range(rows // MM_ROW_CHUNK)]


def _gateup_kernel(h_ref, wg_ref, wu_ref, o_ref, wg16_ref, wu16_ref):
    @pl.when(pl.program_id(1) == 0)
    def _():
        wg16_ref[...] = wg_ref[...].astype(bf16)
        wu16_ref[...] = wu_ref[...].astype(bf16)

    for rows in _row_chunks(h_ref.shape[0]):
        h = h_ref[rows, :]
        g = jnp.dot(h, wg16_ref[...], preferred_element_type=f32)
        u = jnp.dot(h, wu16_ref[...], preferred_element_type=f32)
        o_ref[rows, :] = (g * jax.nn.sigmoid(g) * u).astype(o_ref.dtype)


def gateup(h, wg, wu, layer, tm=2048, tf=512):
    n, d = h.shape
    dff = wg.shape[-1]
    return pl.pallas_call(
        _gateup_kernel,
        grid=(dff // tf, n // tm),
        in_specs=[pl.BlockSpec((tm, d), lambda j, i: (i, 0)),
                  pl.BlockSpec((None, d, tf), lambda j, i: (layer, 0, j)),
                  pl.BlockSpec((None, d, tf), lambda j, i: (layer, 0, j))],
        out_specs=pl.BlockSpec((tm, tf), lambda j, i: (i, j)),
        out_shape=jax.ShapeDtypeStruct((n, dff), bf16),
        scratch_shapes=[pltpu.VMEM((d, tf), bf16), pltpu.VMEM((d, tf), bf16)],
        compiler_params=_params(("arbitrary", "arbitrary"), 56),
        name="ffn_gateup",
    )(h, wg, wu)


def _down_kernel(a_ref, w_ref, x_ref, g_ref, xo_ref, ho_ref):
    acc = jnp.dot(a_ref[...], w_ref[...], preferred_element_type=f32)
    xn = x_ref[...] + 0.5 * acc
    xo_ref[...] = xn
    ho_ref[...] = _rms(xn, g_ref[...]).astype(ho_ref.dtype)


def down_residual_norm(a, wd, layer, x, g_next, h_dtype, tm=256):
    n, dff = a.shape
    d = wd.shape[-1]
    return pl.pallas_call(
        _down_kernel,
        grid=(n // tm,),
        in_specs=[pl.BlockSpec((tm, dff), lambda i: (i, 0)),
                  pl.BlockSpec((None, dff, d), lambda i: (layer, 0, 0), pipeline_mode=pl.Buffered(1)),
                  pl.BlockSpec((tm, d), lambda i: (i, 0)),
                  pl.BlockSpec((1, d), lambda i: (0, 0))],
        out_specs=[pl.BlockSpec((tm, d), lambda i: (i, 0)),
                   pl.BlockSpec((tm, d), lambda i: (i, 0))],
        out_shape=[jax.ShapeDtypeStruct((n, d), f32),
                   jax.ShapeDtypeStruct((n, d), h_dtype)],
        compiler_params=_params(("parallel",), 56),
        name="ffn_down",
    )(a, wd, x, g_next)


def _inproj_kernel(h_ref, w_ref, o_ref, w16_ref):
    @pl.when(pl.program_id(1) == 0)
    def _():
        w16_ref[...] = w_ref[...].astype(bf16)

    for rows in _row_chunks(h_ref.shape[0]):
        o_ref[rows, :] = jnp.dot(h_ref[rows, :], w16_ref[...], preferred_element_type=f32).astype(o_ref.dtype)


def _ingate_kernel(h_ref, w_ref, b_ref, o_ref, w16_ref):
    @pl.when(pl.program_id(1) == 0)
    def _():
        w16_ref[...] = w_ref[...].astype(bf16)

    for rows in _row_chunks(h_ref.shape[0]):
        acc = jnp.dot(h_ref[rows, :], w16_ref[...], preferred_element_type=f32)
        o_ref[rows, :] = jax.nn.sigmoid(acc + b_ref[...]).astype(o_ref.dtype)


def in_projection(h, w_in, layer, tm=2048, tn=1024):
    n, d = h.shape
    return pl.pallas_call(
        _inproj_kernel,
        grid=(Z_W // tn, n // tm),
        in_specs=[pl.BlockSpec((tm, d), lambda j, i: (i, 0)),
                  pl.BlockSpec((None, d, tn), lambda j, i: (layer, 0, j))],
        out_specs=pl.BlockSpec((tm, tn), lambda j, i: (i, j)),
        out_shape=jax.ShapeDtypeStruct((n, Z_W), bf16),
        scratch_shapes=[pltpu.VMEM((d, tn), bf16)],
        compiler_params=_params(("arbitrary", "arbitrary"), 56),
        name="mix_inproj",
    )(h, w_in)


def in_gates(h, w_in, b_gate, layer, tm=2048, tn=1024):
    n, d = h.shape
    off = Z_W // tn
    return pl.pallas_call(
        _ingate_kernel,
        grid=(GATE_W // tn, n // tm),
        in_specs=[pl.BlockSpec((tm, d), lambda j, i: (i, 0)),
                  pl.BlockSpec((None, d, tn), lambda j, i: (layer, 0, j + off)),
                  pl.BlockSpec((None, 1, tn), lambda j, i: (layer, 0, j))],
        out_specs=pl.BlockSpec((tm, tn), lambda j, i: (i, j)),
        out_shape=jax.ShapeDtypeStruct((n, GATE_W), bf16),
        scratch_shapes=[pltpu.VMEM((d, tn), bf16)],
        compiler_params=_params(("arbitrary", "arbitrary"), 56),
        name="mix_ingate",
    )(h, w_in, b_gate)


def _retention_kernel(q_ref, k_ref, v_ref, g_ref, cos_ref, sin_ref, inner_ref, xi_ref, zeta_ref,
                      gch_ref, o_ref, state_ref, *, nchunk):
    @pl.when(pl.program_id(2) == 0)
    def _():
        state_ref[...] = jnp.zeros_like(state_ref)

    inner = inner_ref[...]
    xi = xi_ref[...]
    zeta = zeta_ref[...]
    gch = gch_ref[...]
    half = RET_QK_DIM // 2
    chunks = range(nchunk)
    rows = [pl.ds(c * BLOCK, BLOCK) for c in chunks]

    def rot(t, c):
        cos, sin = cos_ref[rows[c], :], sin_ref[rows[c], :]
        t1, t2 = t[:, :half], t[:, half:]
        return jnp.concatenate([t1 * cos - t2 * sin, t1 * sin + t2 * cos], axis=-1)

    qb = [rot(q_ref[rows[c], :].astype(f32), c).astype(bf16) for c in chunks]
    k = [rot(k_ref[rows[c], :].astype(f32), c) * (RET_QK_DIM ** -0.5) for c in chunks]
    v = [v_ref[rows[c], :] for c in chunks]
    a = [lax.dot_general(qb[c], k[c].astype(bf16), (((1,), (1,)), ((), ())), preferred_element_type=f32) * inner
         for c in chunks]
    upd = [jnp.dot((k[c] * zeta).T.astype(bf16), v[c], preferred_element_type=f32) for c in chunks]
    local = [jnp.dot(a[c].astype(bf16), v[c], preferred_element_type=f32) for c in chunks]
    state = state_ref[...]
    carried = []
    for c in chunks:
        carried.append(jnp.dot(qb[c], state.astype(bf16), preferred_element_type=f32))
        state = state * gch + upd[c]
    state_ref[...] = state
    for c in chunks:
        o = local[c] + carried[c] * xi
        on = o * lax.rsqrt(jnp.mean(o * o, axis=-1, keepdims=True) + EPS)
        gt = g_ref[rows[c], :].astype(f32)
        o_ref[rows[c], :] = (gt * jax.nn.sigmoid(gt) * on).astype(o_ref.dtype)


def _retention_tables(seq):
    h = jnp.arange(RET_HEADS, dtype=f32)
    log_g = jnp.log1p(-(2.0 ** (-5.0 - h)))
    idx = jnp.arange(BLOCK, dtype=f32)
    rel = idx[:, None] - idx[None, :]
    inner = jnp.where(rel[None] >= 0, jnp.exp(jnp.maximum(rel, 0.0)[None] * log_g[:, None, None]), 0.0)
    xi = jnp.exp((idx + 1.0)[None, :] * log_g[:, None])[..., None]
    zeta = jnp.exp((BLOCK - 1.0 - idx)[None, :] * log_g[:, None])[..., None]
    gch = jnp.broadcast_to(jnp.exp(BLOCK * log_g)[:, None, None], (RET_HEADS, 1, RET_V_DIM))
    inv = ROPE_BASE ** (-jnp.arange(0, RET_QK_DIM, 2, dtype=f32) / RET_QK_DIM)
    ang = jnp.arange(seq).astype(f32)[:, None] * inv[None, :]
    return jnp.cos(ang), jnp.sin(ang), inner, xi, zeta, gch


def retention(z, tables, tokens=512):
    b, s, _ = z.shape
    cos, sin, inner, xi, zeta, gch = tables
    nq = RET_QK_W // RET_QK_DIM
    nv = 2 * RET_QK_W // RET_V_DIM
    kern = functools.partial(_retention_kernel, nchunk=tokens // BLOCK)
    return pl.pallas_call(
        kern,
        grid=(b, RET_HEADS, s // tokens),
        in_specs=[pl.BlockSpec((None, tokens, RET_QK_DIM), lambda bi, h, t: (bi, t, h)),
                  pl.BlockSpec((None, tokens, RET_QK_DIM), lambda bi, h, t: (bi, t, nq + h)),
                  pl.BlockSpec((None, tokens, RET_V_DIM), lambda bi, h, t: (bi, t, nv + h)),
                  pl.BlockSpec((None, tokens, RET_V_DIM), lambda bi, h, t: (bi, t, nv + RET_HEADS + h)),
                  pl.BlockSpec((tokens, RET_QK_DIM // 2), lambda bi, h, t: (t, 0)),
                  pl.BlockSpec((tokens, RET_QK_DIM // 2), lambda bi, h, t: (t, 0)),
                  pl.BlockSpec((None, BLOCK, BLOCK), lambda bi, h, t: (h, 0, 0)),
                  pl.BlockSpec((None, BLOCK, 1), lambda bi, h, t: (h, 0, 0)),
                  pl.BlockSpec((None, BLOCK, 1), lambda bi, h, t: (h, 0, 0)),
                  pl.BlockSpec((None, 1, RET_V_DIM), lambda bi, h, t: (h, 0, 0))],
        out_specs=pl.BlockSpec((None, tokens, RET_V_DIM), lambda bi, h, t: (bi, t, h)),
        out_shape=jax.ShapeDtypeStruct((b, s, RET_V_W), bf16),
        scratch_shapes=[pltpu.VMEM((RET_QK_DIM, RET_V_DIM), f32)],
        compiler_params=_params(("parallel", "parallel", "arbitrary"), 32),
        name="mix_retention",
    )(z, z, z, z, cos, sin, inner, xi, zeta, gch)


def _sgu_kernel(u_ref, v_ref, lg_ref, lb_ref, w_ref, b_ref, o_ref, *, nchunk):
    u = jax.nn.gelu(u_ref[...].astype(f32))
    v = jax.nn.gelu(v_ref[...].astype(f32))
    mu = jnp.mean(v, axis=-1, keepdims=True)
    var = jnp.mean(jnp.square(v - mu), axis=-1, keepdims=True)
    vn = ((v - mu) * lax.rsqrt(var + EPS) * lg_ref[...] + lb_ref[...]).astype(bf16)
    row = lax.broadcasted_iota(jnp.int32, (BLOCK, BLOCK), 0)
    col = lax.broadcasted_iota(jnp.int32, (BLOCK, BLOCK), 1)
    causal = row >= col
    for g in range(SG_GROUPS):
        wc = jnp.where(causal, w_ref[g], 0.0).astype(bf16)
        bias = b_ref[g]
        cols = slice(g * SG_GROUP_DIM, (g + 1) * SG_GROUP_DIM)
        for c in range(nchunk):
            rows = slice(c * BLOCK, (c + 1) * BLOCK)
            mix = jnp.dot(wc, vn[rows, cols], preferred_element_type=f32) + bias
            o_ref[rows, cols] = (u[rows, cols] * mix).astype(o_ref.dtype)


def spatial_gating(z2, ln_g, ln_b, sg_w, sg_b, layer, tokens=512):
    n = z2.shape[0]
    off = (2 * RET_QK_W + 2 * RET_V_W) // SG_W
    kern = functools.partial(_sgu_kernel, nchunk=tokens // BLOCK)
    return pl.pallas_call(
        kern,
        grid=(n // tokens,),
        in_specs=[pl.BlockSpec((tokens, SG_W), lambda i: (i, off)),
                  pl.BlockSpec((tokens, SG_W), lambda i: (i, off + 1)),
                  pl.BlockSpec((None, 1, SG_W), lambda i: (layer, 0, 0)),
                  pl.BlockSpec((None, 1, SG_W), lambda i: (layer, 0, 0)),
                  pl.BlockSpec((None, SG_GROUPS, BLOCK, BLOCK), lambda i: (layer, 0, 0, 0)),
                  pl.BlockSpec((None, SG_GROUPS, BLOCK, 1), lambda i: (layer, 0, 0, 0))],
        out_specs=pl.BlockSpec((tokens, SG_W), lambda i: (i, 0)),
        out_shape=jax.ShapeDtypeStruct((n, SG_W), bf16),
        compiler_params=_params(("parallel",), 32),
        name="mix_sgu",
    )(z2, z2, ln_g, ln_b, sg_w, sg_b)


def _padded(t):
    return t + t // ATT_PAD_EVERY


def _attention_kernel(q_ref, k_ref, v_ref, bias_ref, o_ref, kp_ref, vp_ref, qp_ref, stage_ref,
                      mp_ref, lp_ref, accp_ref, mn_ref, ln_ref, accn_ref):
    sb = pl.program_id(2)
    seq = k_ref.shape[0]
    scale = ATT_HEAD_DIM ** -0.5
    quarter = ATT_SUPER // ATT_PAD_EVERY
    super_p = _padded(ATT_SUPER)

    def pad_rows(dst_ref, dst_base):
        for j in range(ATT_PAD_EVERY):
            dst_ref[pl.ds(dst_base + j, quarter, stride=ATT_PAD_EVERY + 1), :] = (
                stage_ref[pl.ds(j, quarter, stride=ATT_PAD_EVERY), :])

    @pl.when(sb == 0)
    def _():
        for src_ref, dst_ref in ((k_ref, kp_ref), (v_ref, vp_ref)):
            for c in range(seq // ATT_SUPER):
                stage_ref[...] = src_ref[pl.ds(c * ATT_SUPER, ATT_SUPER), :].astype(f32)
                pad_rows(dst_ref, c * super_p)

    stage_ref[...] = q_ref[...].astype(f32)
    pad_rows(qp_ref, 0)

    def softmax(s):
        mb = jnp.max(s, axis=-1, keepdims=True)
        p = jnp.exp(s - mb)
        return mb, jnp.sum(p, axis=-1, keepdims=True), p.astype(bf16)

    def merged(state, mb, lb, ob):
        mb = jnp.broadcast_to(mb, (BLOCK, ATT_HEAD_DIM))
        lb = jnp.broadcast_to(lb, (BLOCK, ATT_HEAD_DIM))
        if state is None:
            return mb, lb, ob
        m_old, l_old, acc_old = state
        m_new = jnp.maximum(m_old, mb)
        wa = jnp.exp(m_old - m_new)
        wb = jnp.exp(mb - m_new)
        return m_new, wa * l_old + wb * lb, wa * acc_old + wb * ob

    def run_pass(nblocks, group, load, finish):
        def body(step, carry):
            its = [step * group + g for g in range(group)]
            ins = [load(it) for it in its]
            logit = [lax.dot_general(q, k, (((1,), (1,)), ((), ())), preferred_element_type=f32) * scale + b
                     for q, k, _, b, _ in ins]
            soft = [softmax(s) for s in logit]
            outs = [jnp.dot(soft[g][2], ins[g][2], preferred_element_type=f32) for g in range(group)]
            new = [merged(ins[g][4], soft[g][0], soft[g][1], outs[g]) for g in range(group)]
            for g in range(group):
                finish(its[g], *new[g])
            return carry

        lax.fori_loop(0, nblocks // group, body, 0)

    for bi, (window, d) in reversed(list(enumerate(DILATION_CONFIGS))[1:]):
        assert window // d == BLOCK and ATT_SUPER % (d * BLOCK) == 0 and d % ATT_PAD_EVERY == 0
        span_p = _padded(d * BLOCK)
        blocks_per_super = ATT_SUPER // (d * BLOCK)
        shift = d.bit_length() - 1
        stride = _padded(d)
        is_first_pass = bi == len(DILATION_CONFIGS) - 1

        def rows(start, stride=stride):
            return pl.ds(start, BLOCK, stride=stride)

        def load(it, bi=bi, d=d, span_p=span_p, blocks_per_super=blocks_per_super, shift=shift,
                 rows=rows, is_first_pass=is_first_pass):
            n_loc = lax.shift_right_logical(it, shift)
            r = lax.bitwise_and(it, d - 1)
            r_p = r + lax.shift_right_logical(r, 2)
            n_glob = sb * blocks_per_super + n_loc
            q_rows = rows(n_loc * span_p + r_p)
            cur = n_glob * span_p + r_p
            prev = jnp.maximum(n_glob - 1, 0) * span_p + r_p
            k = jnp.concatenate([kp_ref[rows(prev), :], kp_ref[rows(cur), :]], axis=0).astype(bf16)
            v = jnp.concatenate([vp_ref[rows(prev), :], vp_ref[rows(cur), :]], axis=0).astype(bf16)
            first = (n_glob == 0).astype(jnp.int32)
            state = None if is_first_pass else (mp_ref[q_rows, :], lp_ref[q_rows, :], accp_ref[q_rows, :])
            return qp_ref[q_rows, :].astype(bf16), k, v, bias_ref[bi, first], state

        def finish(it, m, l, acc, d=d, span_p=span_p, shift=shift, rows=rows):
            n_loc = lax.shift_right_logical(it, shift)
            r = lax.bitwise_and(it, d - 1)
            q_rows = rows(n_loc * span_p + r + lax.shift_right_logical(r, 2))
            mp_ref[q_rows, :], lp_ref[q_rows, :], accp_ref[q_rows, :] = m, l, acc

        run_pass(blocks_per_super * d, ATT_GROUP_FIRST if is_first_pass else ATT_GROUP, load, finish)

    for src_ref, dst_ref in ((mp_ref, mn_ref), (lp_ref, ln_ref), (accp_ref, accn_ref)):
        for j in range(ATT_PAD_EVERY):
            dst_ref[pl.ds(j, quarter, stride=ATT_PAD_EVERY), :] = (
                src_ref[pl.ds(j, quarter, stride=ATT_PAD_EVERY + 1), :])

    window, d = DILATION_CONFIGS[0]
    assert d == 1 and window == BLOCK

    def load_dense(it):
        n_glob = sb * (ATT_SUPER // BLOCK) + it
        q_rows = pl.ds(pl.multiple_of(it * BLOCK, BLOCK), BLOCK)
        cur = pl.ds(pl.multiple_of(n_glob * BLOCK, BLOCK), BLOCK)
        prev = pl.ds(pl.multiple_of(jnp.maximum(n_glob - 1, 0) * BLOCK, BLOCK), BLOCK)
        k = jnp.concatenate([k_ref[prev, :], k_ref[cur, :]], axis=0)
        v = jnp.concatenate([v_ref[prev, :], v_ref[cur, :]], axis=0)
        first = (n_glob == 0).astype(jnp.int32)
        return q_ref[q_rows, :], k, v, bias_ref[0, first], (mn_ref[q_rows, :], ln_ref[q_rows, :], accn_ref[q_rows, :])

    def finish_dense(it, m, l, acc):
        o_ref[pl.ds(pl.multiple_of(it * BLOCK, BLOCK), BLOCK), :] = (acc / l).astype(o_ref.dtype)

    run_pass(ATT_SUPER // BLOCK, ATT_GROUP, load_dense, finish_dense)


def _t5_bucket(dist):
    max_exact = REL_BUCKETS // 2
    d_f = jnp.maximum(dist, 1).astype(f32)
    large = max_exact + (jnp.log(d_f / max_exact) / math.log(REL_MAX_DIST / max_exact)
                         * (REL_BUCKETS - max_exact)).astype(jnp.int32)
    large = jnp.minimum(large, REL_BUCKETS - 1)
    return jnp.where(dist < max_exact, dist, large)


def _attention_bias(rel_bias):
    i = jnp.arange(BLOCK)[:, None]
    j = jnp.arange(2 * BLOCK)[None, :]
    steps = BLOCK + i - j
    band = (steps >= 0) & (steps <= BLOCK)
    tab = rel_bias.astype(f32)
    out = []
    for _, d in DILATION_CONFIGS:
        onehot = jax.nn.one_hot(_t5_bucket(d * jnp.maximum(steps, 0)), REL_BUCKETS, dtype=f32)
        bias = jnp.einsum('qkb,bh->hqk', onehot, tab, precision=lax.Precision.HIGHEST)
        later = jnp.where(band[None], bias, NEG_INF)
        first = jnp.where((band & (j >= BLOCK))[None], bias, NEG_INF)
        out.append(jnp.stack([later, first], axis=1))
    return jnp.stack(out)


def dilated_attention(z, bias):
    b, s, _ = z.shape
    off = (2 * RET_QK_W + 2 * RET_V_W + 2 * SG_W) // ATT_HEAD_DIM
    return pl.pallas_call(
        _attention_kernel,
        grid=(b, ATT_HEADS, s // ATT_SUPER),
        in_specs=[pl.BlockSpec((None, ATT_SUPER, ATT_HEAD_DIM), lambda bi, h, t: (bi, t, off + h)),
                  pl.BlockSpec((None, s, ATT_HEAD_DIM), lambda bi, h, t: (bi, 0, off + ATT_HEADS + h)),
                  pl.BlockSpec((None, s, ATT_HEAD_DIM), lambda bi, h, t: (bi, 0, off + 2 * ATT_HEADS + h)),
                  pl.BlockSpec((N_BRANCH, None, 2, BLOCK, 2 * BLOCK), lambda bi, h, t: (0, h, 0, 0, 0))],
        out_specs=pl.BlockSpec((None, ATT_SUPER, ATT_HEAD_DIM), lambda bi, h, t: (bi, t, h)),
        out_shape=jax.ShapeDtypeStruct((b, s, ATT_W), bf16),
        scratch_shapes=([pltpu.VMEM((_padded(s), ATT_HEAD_DIM), f32)] * 2
                        + [pltpu.VMEM((_padded(ATT_SUPER), ATT_HEAD_DIM), f32)]
                        + [pltpu.VMEM((ATT_SUPER, ATT_HEAD_DIM), f32)]
                        + [pltpu.VMEM((_padded(ATT_SUPER), ATT_HEAD_DIM), f32)] * 3
                        + [pltpu.VMEM((ATT_SUPER, ATT_HEAD_DIM), f32)] * 3),
        compiler_params=_params(("parallel", "parallel", "arbitrary"), 40),
        name="mix_attention",
    )(z, z, z, bias)


def _merge_kernel(ret_ref, sg_ref, att_ref, gate_ref, wr_ref, ws_ref, wa_ref, wo_ref, x_ref, g_ref,
                  xo_ref, ho_ref):
    d = D_MODEL
    merged = gate_ref[:, 0:d].astype(f32) * jnp.dot(ret_ref[...], wr_ref[...], preferred_element_type=f32)
    merged += gate_ref[:, d:2 * d].astype(f32) * jnp.dot(sg_ref[...], ws_ref[...], preferred_element_type=f32)
    merged += gate_ref[:, 2 * d:3 * d].astype(f32) * jnp.dot(att_ref[...], wa_ref[...], preferred_element_type=f32)
    xn = x_ref[...] + jnp.dot(merged.astype(bf16), wo_ref[...], preferred_element_type=f32)
    xo_ref[...] = xn
    ho_ref[...] = _rms(xn, g_ref[...]).astype(ho_ref.dtype)


def merge_project(ret, sg, att, gates, w_ret, w_sg, w_att, w_out, layer, x, g_next, tm=256):
    n, d = x.shape

    def resident(w):
        return pl.BlockSpec((None,) + w.shape[1:], lambda i: (layer, 0, 0), pipeline_mode=pl.Buffered(1))

    def rows(w):
        return pl.BlockSpec((tm, w), lambda i: (i, 0))

    return pl.pallas_call(
        _merge_kernel,
        grid=(n // tm,),
        in_specs=[rows(RET_V_W), rows(SG_W), rows(ATT_W), rows(GATE_W),
                  resident(w_ret), resident(w_sg), resident(w_att), resident(w_out),
                  rows(d), pl.BlockSpec((1, d), lambda i: (0, 0))],
        out_specs=[rows(d), rows(d)],
        out_shape=[jax.ShapeDtypeStruct((n, d), f32), jax.ShapeDtypeStruct((n, d), bf16)],
        compiler_params=_params(("parallel",), 60),
        name="mix_merge",
    )(ret, sg, att, gates, w_ret, w_sg, w_att, w_out, x, g_next)


def kernel(x, ffn1_norm, ffn1_w_gate, ffn1_w_up, ffn1_w_down, mix_norm, w_in, b_gate, sg_ln_g, sg_ln_b,
           sg_w, sg_b, rel_bias, w_proj_ret, w_proj_sg, w_proj_att, w_out, ffn2_norm, ffn2_w_gate,
           ffn2_w_up, ffn2_w_down, final_norm):
    b, s, d = x.shape
    n = b * s
    depth = ffn1_norm.shape[0]

    def w16(w):
        return w.astype(bf16)

    ffn1_wd, ffn2_wd = w16(ffn1_w_down), w16(ffn2_w_down)
    w_ret16, w_sg16, w_att16, w_out16 = w16(w_proj_ret), w16(w_proj_sg), w16(w_proj_att), w16(w_out)
    b_gate3 = b_gate[:, None, :]
    ln_g3, ln_b3 = sg_ln_g[:, None, :], sg_ln_b[:, None, :]
    sg_b4 = sg_b[..., None]
    tables = _retention_tables(s)
    att_bias = _attention_bias(rel_bias)

    xr = x.reshape(n, d)
    h = rmsnorm(xr, ffn1_norm[0][None], bf16)
    for l in range(depth):
        a = gateup(h, ffn1_w_gate, ffn1_w_up, l)
        xr, h = down_residual_norm(a, ffn1_wd, l, xr, mix_norm[l][None], bf16)

        z = in_projection(h, w_in, l)
        gates = in_gates(h, w_in, b_gate3, l)
        z3 = z.reshape(b, s, Z_W)
        ret = retention(z3, tables).reshape(n, RET_V_W)
        sg = spatial_gating(z, ln_g3, ln_b3, sg_w, sg_b4, l)
        att = dilated_attention(z3, att_bias).reshape(n, ATT_W)
        xr, h = merge_project(ret, sg, att, gates, w_ret16, w_sg16, w_att16, w_out16, l, xr,
                              ffn2_norm[l][None])

        a = gateup(h, ffn2_w_gate, ffn2_w_up, l)
        last = l == depth - 1
        g_next = final_norm[None] if last else ffn1_norm[l + 1][None]
        xr, h = down_residual_norm(a, ffn2_wd, l, xr, g_next, f32 if last else bf16)
    return h.reshape(b, s, d)
```

```python
import functools
import math

import jax
import jax.numpy as jnp
from jax import lax
from jax.experimental import pallas as pl
from jax.experimental.pallas import tpu as pltpu

f32 = jnp.float32
bf16 = jnp.bfloat16

D_MODEL = 2048
DEPTH = 4
BLOCK = 128
RET_HEADS = 4
RET_QK_DIM = 256
RET_V_DIM = 512
SG_GROUPS = 4
SG_GROUP_DIM = 256
ATT_HEADS = 8
ATT_HEAD_DIM = 128
DILATION_CONFIGS = ((128, 1), (512, 4), (2048, 16))
REL_BUCKETS = 32
REL_MAX_DIST = 2048
D_FF = 5632
ROPE_BASE = 10000.0
EPS = 1e-6
NEG_INF = -1e30

RET_QK_W = RET_HEADS * RET_QK_DIM
RET_V_W = RET_HEADS * RET_V_DIM
SG_W = SG_GROUPS * SG_GROUP_DIM
ATT_W = ATT_HEADS * ATT_HEAD_DIM
N_BRANCH = 3
Z_W = 2 * RET_QK_W + 2 * RET_V_W + 2 * SG_W + 3 * ATT_W
GATE_W = N_BRANCH * D_MODEL

ATT_SUPER = 2048
ATT_GROUP = 4
ATT_GROUP_FIRST = 8
ATT_PAD_EVERY = 4
MM_ROW_CHUNK = 256
V7X_VMEM_LIMIT = 60 * 1024 * 1024


def _params(sem, vmem_mb=None):
    return pltpu.CompilerParams(
        dimension_semantics=sem,
        vmem_limit_bytes=None if vmem_mb is None else min(vmem_mb * 1024 * 1024, V7X_VMEM_LIMIT))


def _rms(x, g):
    return x * lax.rsqrt(jnp.mean(x * x, axis=-1, keepdims=True) + EPS) * g


def _rmsnorm_kernel(x_ref, g_ref, o_ref):
    o_ref[...] = _rms(x_ref[...], g_ref[...]).astype(o_ref.dtype)


def rmsnorm(x, g, out_dtype, tm=512):
    n, d = x.shape
    return pl.pallas_call(
        _rmsnorm_kernel,
        grid=(n // tm,),
        in_specs=[pl.BlockSpec((tm, d), lambda i: (i, 0)),
                  pl.BlockSpec((1, d), lambda i: (0, 0))],
        out_specs=pl.BlockSpec((tm, d), lambda i: (i, 0)),
        out_shape=jax.ShapeDtypeStruct((n, d), out_dtype),
        compiler_params=_params(("parallel",)),
        name="rmsnorm",
    )(x, g)


def _row_chunks(rows):
    return [pl.ds(c * MM_ROW_CHUNK, MM_ROW_CHUNK) for c in range(rows // MM_ROW_CHUNK)]


def _gateup_kernel(h_ref, wg_ref, wu_ref, o_ref, wg16_ref, wu16_ref):
    @pl.when(pl.program_id(1) == 0)
    def _():
        wg16_ref[...] = wg_ref[...].astype(bf16)
        wu16_ref[...] = wu_ref[...].astype(bf16)

    for rows in _row_chunks(h_ref.shape[0]):
        h = h_ref[rows, :]
        g = jnp.dot(h, wg16_ref[...], preferred_element_type=f32)
        u = jnp.dot(h, wu16_ref[...], preferred_element_type=f32)
        o_ref[rows, :] = (g * jax.nn.sigmoid(g) * u).astype(o_ref.dtype)


def gateup(h, wg, wu, layer, tm=2048, tf=512):
    n, d = h.shape
    dff = wg.shape[-1]
    return pl.pallas_call(
        _gateup_kernel,
        grid=(dff // tf, n // tm),
        in_specs=[pl.BlockSpec((tm, d), lambda j, i: (i, 0)),
                  pl.BlockSpec((None, d, tf), lambda j, i: (layer, 0, j)),
                  pl.BlockSpec((None, d, tf), lambda j, i: (layer, 0, j))],
        out_specs=pl.BlockSpec((tm, tf), lambda j, i: (i, j)),
        out_shape=jax.ShapeDtypeStruct((n, dff), bf16),
        scratch_shapes=[pltpu.VMEM((d, tf), bf16), pltpu.VMEM((d, tf), bf16)],
        compiler_params=_params(("arbitrary", "arbitrary"), 56),
        name="ffn_gateup",
    )(h, wg, wu)


def _down_kernel(a_ref, w_ref, x_ref, g_ref, xo_ref, ho_ref):
    acc = jnp.dot(a_ref[...], w_ref[...], preferred_element_type=f32)
    xn = x_ref[...] + 0.5 * acc
    xo_ref[...] = xn
    ho_ref[...] = _rms(xn, g_ref[...]).astype(ho_ref.dtype)


def down_residual_norm(a, wd, layer, x, g_next, h_dtype, tm=256):
    n, dff = a.shape
    d = wd.shape[-1]
    return pl.pallas_call(
        _down_kernel,
        grid=(n // tm,),
        in_specs=[pl.BlockSpec((tm, dff), lambda i: (i, 0)),
                  pl.BlockSpec((None, dff, d), lambda i: (layer, 0, 0), pipeline_mode=pl.Buffered(1)),
                  pl.BlockSpec((tm, d), lambda i: (i, 0)),
                  pl.BlockSpec((1, d), lambda i: (0, 0))],
        out_specs=[pl.BlockSpec((tm, d), lambda i: (i, 0)),
                   pl.BlockSpec((tm, d), lambda i: (i, 0))],
        out_shape=[jax.ShapeDtypeStruct((n, d), f32),
                   jax.ShapeDtypeStruct((n, d), h_dtype)],
        compiler_params=_params(("parallel",), 56),
        name="ffn_down",
    )(a, wd, x, g_next)


def _stream_project(h_ref, w_ref, w16_ref, o_ref, epilogue):
    @pl.when(pl.program_id(1) == 0)
    def _():
        w16_ref[...] = w_ref[...].astype(bf16)

    for rows in _row_chunks(h_ref.shape[0]):
        acc = jnp.dot(h_ref[rows, :], w16_ref[...], preferred_element_type=f32)
        o_ref[rows, :] = epilogue(rows, acc).astype(o_ref.dtype)


def _inproj_plain_kernel(h_ref, w_ref, o_ref, w16_ref):
    _stream_project(h_ref, w_ref, w16_ref, o_ref, lambda rows, acc: acc)


def _inproj_silu_kernel(h_ref, w_ref, o_ref, w16_ref):
    _stream_project(h_ref, w_ref, w16_ref, o_ref, lambda rows, acc: acc * jax.nn.sigmoid(acc))


def _inproj_gelu_kernel(h_ref, w_ref, o_ref, w16_ref):
    _stream_project(h_ref, w_ref, w16_ref, o_ref, lambda rows, acc: jax.nn.gelu(acc))


def _inproj_rotary_kernel(h_ref, w_ref, cos_ref, sin_ref, o_ref, w16_ref):
    kscale = jnp.where(pl.program_id(0) == 1, RET_QK_DIM ** -0.5, 1.0).astype(f32)
    half = RET_QK_DIM // 2

    def rotate(rows, acc):
        cos = cos_ref[rows, :] * kscale
        sin = sin_ref[rows, :] * kscale
        out = []
        for hd in range(acc.shape[1] // RET_QK_DIM):
            t1 = acc[:, hd * RET_QK_DIM:hd * RET_QK_DIM + half]
            t2 = acc[:, hd * RET_QK_DIM + half:(hd + 1) * RET_QK_DIM]
            out += [t1 * cos - t2 * sin, t1 * sin + t2 * cos]
        return jnp.concatenate(out, axis=-1)

    _stream_project(h_ref, w_ref, w16_ref, o_ref, rotate)


def _ingate_kernel(h_ref, w_ref, b_ref, o_ref, w16_ref):
    _stream_project(h_ref, w_ref, w16_ref, o_ref, lambda rows, acc: jax.nn.sigmoid(acc + b_ref[...]))


def in_projection(kern, name, h, w_in, layer, col0, width, extra=(), extra_specs=(), tm=2048, tn=1024):
    n, d = h.shape
    off = col0 // tn
    return pl.pallas_call(
        kern,
        grid=(width // tn, n // tm),
        in_specs=[pl.BlockSpec((tm, d), lambda j, i: (i, 0)),
                  pl.BlockSpec((None, d, tn), lambda j, i: (layer, 0, j + off)),
                  *extra_specs],
        out_specs=pl.BlockSpec((tm, tn), lambda j, i: (i, j)),
        out_shape=jax.ShapeDtypeStruct((n, width), bf16),
        scratch_shapes=[pltpu.VMEM((d, tn), bf16)],
        compiler_params=_params(("arbitrary", "arbitrary"), 58),
        name=name,
    )(h, w_in, *extra)


def mixer_inputs(h, w_in, b_gate, layer, cos, sin, tm=2048, tn=1024):
    seq_tiles = cos.shape[0] // tm
    table = pl.BlockSpec((tm, RET_QK_DIM // 2), lambda j, i: (i % seq_tiles, 0))
    bias = pl.BlockSpec((None, 1, tn), lambda j, i: (layer, 0, j))
    col = 0
    qk = in_projection(_inproj_rotary_kernel, "mix_in_qk", h, w_in, layer, col, 2 * RET_QK_W,
                       (cos, sin), (table, table))
    col += 2 * RET_QK_W
    rv = in_projection(_inproj_plain_kernel, "mix_in_rv", h, w_in, layer, col, RET_V_W)
    col += RET_V_W
    rg = in_projection(_inproj_silu_kernel, "mix_in_rg", h, w_in, layer, col, RET_V_W)
    col += RET_V_W
    sg = in_projection(_inproj_gelu_kernel, "mix_in_sg", h, w_in, layer, col, 2 * SG_W)
    col += 2 * SG_W
    att = in_projection(_inproj_plain_kernel, "mix_in_att", h, w_in, layer, col, 3 * ATT_W)
    col += 3 * ATT_W
    gates = in_projection(_ingate_kernel, "mix_in_gate", h, w_in, layer, col, GATE_W, (b_gate,), (bias,))
    return qk, rv, rg, sg, att, gates


def _retention_kernel(q_ref, k_ref, v_ref, g_ref, inner_ref, xi_ref, zeta_ref, gch_ref, o_ref, state_ref,
                      *, nchunk):
    @pl.when(pl.program_id(2) == 0)
    def _():
        state_ref[...] = jnp.zeros_like(state_ref)

    inner = inner_ref[...]
    xi = xi_ref[...]
    zeta = zeta_ref[...]
    gch = gch_ref[...]
    chunks = range(nchunk)
    rows = [pl.ds(c * BLOCK, BLOCK) for c in chunks]

    q = [q_ref[rows[c], :] for c in chunks]
    k = [k_ref[rows[c], :] for c in chunks]
    v = [v_ref[rows[c], :] for c in chunks]
    a = [lax.dot_general(q[c], k[c], (((1,), (1,)), ((), ())), preferred_element_type=f32) * inner
         for c in chunks]
    upd = [jnp.dot((k[c].astype(f32) * zeta).T.astype(bf16), v[c], preferred_element_type=f32)
           for c in chunks]
    local = [jnp.dot(a[c].astype(bf16), v[c], preferred_element_type=f32) for c in chunks]
    state = state_ref[...]
    carried = []
    for c in chunks:
        carried.append(jnp.dot(q[c], state.astype(bf16), preferred_element_type=f32))
        state = state * gch + upd[c]
    state_ref[...] = state
    for c in chunks:
        o = local[c] + carried[c] * xi
        on = o * lax.rsqrt(jnp.mean(o * o, axis=-1, keepdims=True) + EPS)
        o_ref[rows[c], :] = (g_ref[rows[c], :].astype(f32) * on).astype(o_ref.dtype)


def _retention_tables(seq):
    h = jnp.arange(RET_HEADS, dtype=f32)
    log_g = jnp.log1p(-(2.0 ** (-5.0 - h)))
    idx = jnp.arange(BLOCK, dtype=f32)
    rel = idx[:, None] - idx[None, :]
    inner = jnp.where(rel[None] >= 0, jnp.exp(jnp.maximum(rel, 0.0)[None] * log_g[:, None, None]), 0.0)
    xi = jnp.exp((idx + 1.0)[None, :] * log_g[:, None])[..., None]
    zeta = jnp.exp((BLOCK - 1.0 - idx)[None, :] * log_g[:, None])[..., None]
    gch = jnp.broadcast_to(jnp.exp(BLOCK * log_g)[:, None, None], (RET_HEADS, 1, RET_V_DIM))
    return inner, xi, zeta, gch


def _rotary_tables(seq):
    inv = ROPE_BASE ** (-jnp.arange(0, RET_QK_DIM, 2, dtype=f32) / RET_QK_DIM)
    ang = jnp.arange(seq).astype(f32)[:, None] * inv[None, :]
    return jnp.cos(ang), jnp.sin(ang)


def retention(qk, rv, rg, tables, tokens=512):
    b, s, _ = qk.shape
    inner, xi, zeta, gch = tables
    kern = functools.partial(_retention_kernel, nchunk=tokens // BLOCK)
    return pl.pallas_call(
        kern,
        grid=(b, RET_HEADS, s // tokens),
        in_specs=[pl.BlockSpec((None, tokens, RET_QK_DIM), lambda bi, h, t: (bi, t, h)),
                  pl.BlockSpec((None, tokens, RET_QK_DIM), lambda bi, h, t: (bi, t, RET_HEADS + h)),
                  pl.BlockSpec((None, tokens, RET_V_DIM), lambda bi, h, t: (bi, t, h)),
                  pl.BlockSpec((None, tokens, RET_V_DIM), lambda bi, h, t: (bi, t, h)),
                  pl.BlockSpec((None, BLOCK, BLOCK), lambda bi, h, t: (h, 0, 0)),
                  pl.BlockSpec((None, BLOCK, 1), lambda bi, h, t: (h, 0, 0)),
                  pl.BlockSpec((None, BLOCK, 1), lambda bi, h, t: (h, 0, 0)),
                  pl.BlockSpec((None, 1, RET_V_DIM), lambda bi, h, t: (h, 0, 0))],
        out_specs=pl.BlockSpec((None, tokens, RET_V_DIM), lambda bi, h, t: (bi, t, h)),
        out_shape=jax.ShapeDtypeStruct((b, s, RET_V_W), bf16),
        scratch_shapes=[pltpu.VMEM((RET_QK_DIM, RET_V_DIM), f32)],
        compiler_params=_params(("parallel", "parallel", "arbitrary"), 32),
        name="mix_retention",
    )(qk, qk, rv, rg, inner, xi, zeta, gch)


def _sgu_kernel(u_ref, v_ref, lg_ref, lb_ref, w_ref, b_ref, o_ref, *, nchunk):
    u = u_ref[...].astype(f32)
    v = v_ref[...].astype(f32)
    mu = jnp.mean(v, axis=-1, keepdims=True)
    var = jnp.mean(jnp.square(v - mu), axis=-1, keepdims=True)
    vn = ((v - mu) * lax.rsqrt(var + EPS) * lg_ref[...] + lb_ref[...]).astype(bf16)
    row = lax.broadcasted_iota(jnp.int32, (BLOCK, BLOCK), 0)
    col = lax.broadcasted_iota(jnp.int32, (BLOCK, BLOCK), 1)
    causal = row >= col
    for g in range(SG_GROUPS):
        wc = jnp.where(causal, w_ref[g], 0.0).astype(bf16)
        bias = b_ref[g]
        cols = slice(g * SG_GROUP_DIM, (g + 1) * SG_GROUP_DIM)
        for c in range(nchunk):
            rows = slice(c * BLOCK, (c + 1) * BLOCK)
            mix = jnp.dot(wc, vn[rows, cols], preferred_element_type=f32) + bias
            o_ref[rows, cols] = (u[rows, cols] * mix).astype(o_ref.dtype)


def spatial_gating(sg_in, ln_g, ln_b, sg_w, sg_b, layer, tokens=512):
    n = sg_in.shape[0]
    kern = functools.partial(_sgu_kernel, nchunk=tokens // BLOCK)
    return pl.pallas_call(
        kern,
        grid=(n // tokens,),
        in_specs=[pl.BlockSpec((tokens, SG_W), lambda i: (i, 0)),
                  pl.BlockSpec((tokens, SG_W), lambda i: (i, 1)),
                  pl.BlockSpec((None, 1, SG_W), lambda i: (layer, 0, 0)),
                  pl.BlockSpec((None, 1, SG_W), lambda i: (layer, 0, 0)),
                  pl.BlockSpec((None, SG_GROUPS, BLOCK, BLOCK), lambda i: (layer, 0, 0, 0)),
                  pl.BlockSpec((None, SG_GROUPS, BLOCK, 1), lambda i: (layer, 0, 0, 0))],
        out_specs=pl.BlockSpec((tokens, SG_W), lambda i: (i, 0)),
        out_shape=jax.ShapeDtypeStruct((n, SG_W), bf16),
        compiler_params=_params(("parallel",), 32),
        name="mix_sgu",
    )(sg_in, sg_in, ln_g, ln_b, sg_w, sg_b)


def _padded(t):
    return t + t // ATT_PAD_EVERY


def _attention_kernel(q_ref, k_ref, v_ref, bias_ref, o_ref, kp_ref, vp_ref, qp_ref, stage_ref,
                      mp_ref, lp_ref, accp_ref, mn_ref, ln_ref, accn_ref):
    sb = pl.program_id(2)
    seq = k_ref.shape[0]
    scale = ATT_HEAD_DIM ** -0.5
    quarter = ATT_SUPER // ATT_PAD_EVERY
    super_p = _padded(ATT_SUPER)

    def pad_rows(dst_ref, dst_base):
        for j in range(ATT_PAD_EVERY):
            dst_ref[pl.ds(dst_base + j, quarter, stride=ATT_PAD_EVERY + 1), :] = (
                stage_ref[pl.ds(j, quarter, stride=ATT_PAD_EVERY), :])

    @pl.when(sb == 0)
    def _():
        for src_ref, dst_ref in ((k_ref, kp_ref), (v_ref, vp_ref)):
            for c in range(seq // ATT_SUPER):
                stage_ref[...] = src_ref[pl.ds(c * ATT_SUPER, ATT_SUPER), :].astype(f32)
                pad_rows(dst_ref, c * super_p)

    stage_ref[...] = q_ref[...].astype(f32)
    pad_rows(qp_ref, 0)

    def softmax(s):
        mb = jnp.max(s, axis=-1, keepdims=True)
        p = jnp.exp(s - mb)
        return mb, jnp.sum(p, axis=-1, keepdims=True), p.astype(bf16)

    def merged(state, mb, lb, ob):
        mb = jnp.broadcast_to(mb, (BLOCK, ATT_HEAD_DIM))
        lb = jnp.broadcast_to(lb, (BLOCK, ATT_HEAD_DIM))
        if state is None:
            return mb, lb, ob
        m_old, l_old, acc_old = state
        m_new = jnp.maximum(m_old, mb)
        wa = jnp.exp(m_old - m_new)
        wb = jnp.exp(mb - m_new)
        return m_new, wa * l_old + wb * lb, wa * acc_old + wb * ob

    def run_pass(nblocks, group, load, finish):
        def body(step, carry):
            its = [step * group + g for g in range(group)]
            ins = [load(it) for it in its]
            logit = [lax.dot_general(q, k, (((1,), (1,)), ((), ())), preferred_element_type=f32) * scale + b
                     for q, k, _, b, _ in ins]
            soft = [softmax(s) for s in logit]
            outs = [jnp.dot(soft[g][2], ins[g][2], preferred_element_type=f32) for g in range(group)]
            new = [merged(ins[g][4], soft[g][0], soft[g][1], outs[g]) for g in range(group)]
            for g in range(group):
                finish(its[g], *new[g])
            return carry

        lax.fori_loop(0, nblocks // group, body, 0)

    for bi, (window, d) in reversed(list(enumerate(DILATION_CONFIGS))[1:]):
        assert window // d == BLOCK and ATT_SUPER % (d * BLOCK) == 0 and d % ATT_PAD_EVERY == 0
        span_p = _padded(d * BLOCK)
        blocks_per_super = ATT_SUPER // (d * BLOCK)
        shift = d.bit_length() - 1
        stride = _padded(d)
        is_first_pass = bi == len(DILATION_CONFIGS) - 1

        def rows(start, stride=stride):
            return pl.ds(start, BLOCK, stride=stride)

        def load(it, bi=bi, d=d, span_p=span_p, blocks_per_super=blocks_per_super, shift=shift,
                 rows=rows, is_first_pass=is_first_pass):
            n_loc = lax.shift_right_logical(it, shift)
            r = lax.bitwise_and(it, d - 1)
            r_p = r + lax.shift_right_logical(r, 2)
            n_glob = sb * blocks_per_super + n_loc
            q_rows = rows(n_loc * span_p + r_p)
            cur = n_glob * span_p + r_p
            prev = jnp.maximum(n_glob - 1, 0) * span_p + r_p
            k = jnp.concatenate([kp_ref[rows(prev), :], kp_ref[rows(cur), :]], axis=0).astype(bf16)
            v = jnp.concatenate([vp_ref[rows(prev), :], vp_ref[rows(cur), :]], axis=0).astype(bf16)
            first = (n_glob == 0).astype(jnp.int32)
            state = None if is_first_pass else (mp_ref[q_rows, :], lp_ref[q_rows, :], accp_ref[q_rows, :])
            return qp_ref[q_rows, :].astype(bf16), k, v, bias_ref[bi, first], state

        def finish(it, m, l, acc, bi=bi, d=d, span_p=span_p, shift=shift, rows=rows):
            n_loc = lax.shift_right_logical(it, shift)
            r = lax.bitwise_and(it, d - 1)
            if bi == 1:
                q_rows = pl.ds(n_loc * (d * BLOCK) + r, BLOCK, stride=d)
                mn_ref[q_rows, :], ln_ref[q_rows, :], accn_ref[q_rows, :] = m, l, acc
            else:
                q_rows = rows(n_loc * span_p + r + lax.shift_right_logical(r, 2))
                mp_ref[q_rows, :], lp_ref[q_rows, :], accp_ref[q_rows, :] = m, l, acc

        run_pass(blocks_per_super * d, ATT_GROUP_FIRST if is_first_pass else ATT_GROUP, load, finish)

    window, d = DILATION_CONFIGS[0]
    assert d == 1 and window == BLOCK

    def load_dense(it):
        n_glob = sb * (ATT_SUPER // BLOCK) + it
        q_rows = pl.ds(pl.multiple_of(it * BLOCK, BLOCK), BLOCK)
        cur = pl.ds(pl.multiple_of(n_glob * BLOCK, BLOCK), BLOCK)
        prev = pl.ds(pl.multiple_of(jnp.maximum(n_glob - 1, 0) * BLOCK, BLOCK), BLOCK)
        k = jnp.concatenate([k_ref[prev, :], k_ref[cur, :]], axis=0)
        v = jnp.concatenate([v_ref[prev, :], v_ref[cur, :]], axis=0)
        first = (n_glob == 0).astype(jnp.int32)
        return q_ref[q_rows, :], k, v, bias_ref[0, first], (mn_ref[q_rows, :], ln_ref[q_rows, :], accn_ref[q_rows, :])

    def finish_dense(it, m, l, acc):
        o_ref[pl.ds(pl.multiple_of(it * BLOCK, BLOCK), BLOCK), :] = (acc / l).astype(o_ref.dtype)

    run_pass(ATT_SUPER // BLOCK, ATT_GROUP, load_dense, finish_dense)


def _t5_bucket(dist):
    max_exact = REL_BUCKETS // 2
    d_f = jnp.maximum(dist, 1).astype(f32)
    large = max_exact + (jnp.log(d_f / max_exact) / math.log(REL_MAX_DIST / max_exact)
                         * (REL_BUCKETS - max_exact)).astype(jnp.int32)
    large = jnp.minimum(large, REL_BUCKETS - 1)
    return jnp.where(dist < max_exact, dist, large)


def _attention_bias(rel_bias):
    i = jnp.arange(BLOCK)[:, None]
    j = jnp.arange(2 * BLOCK)[None, :]
    steps = BLOCK + i - j
    band = (steps >= 0) & (steps <= BLOCK)
    tab = rel_bias.astype(f32)
    out = []
    for _, d in DILATION_CONFIGS:
        onehot = jax.nn.one_hot(_t5_bucket(d * jnp.maximum(steps, 0)), REL_BUCKETS, dtype=f32)
        bias = jnp.einsum('qkb,bh->hqk', onehot, tab, precision=lax.Precision.HIGHEST)
        later = jnp.where(band[None], bias, NEG_INF)
        first = jnp.where((band & (j >= BLOCK))[None], bias, NEG_INF)
        out.append(jnp.stack([later, first], axis=1))
    return jnp.stack(out)


def dilated_attention(qkv, bias):
    b, s, _ = qkv.shape
    return pl.pallas_call(
        _attention_kernel,
        grid=(b, ATT_HEADS, s // ATT_SUPER),
        in_specs=[pl.BlockSpec((None, ATT_SUPER, ATT_HEAD_DIM), lambda bi, h, t: (bi, t, h)),
                  pl.BlockSpec((None, s, ATT_HEAD_DIM), lambda bi, h, t: (bi, 0, ATT_HEADS + h)),
                  pl.BlockSpec((None, s, ATT_HEAD_DIM), lambda bi, h, t: (bi, 0, 2 * ATT_HEADS + h)),
                  pl.BlockSpec((N_BRANCH, None, 2, BLOCK, 2 * BLOCK), lambda bi, h, t: (0, h, 0, 0, 0))],
        out_specs=pl.BlockSpec((None, ATT_SUPER, ATT_HEAD_DIM), lambda bi, h, t: (bi, t, h)),
        out_shape=jax.ShapeDtypeStruct((b, s, ATT_W), bf16),
        scratch_shapes=([pltpu.VMEM((_padded(s), ATT_HEAD_DIM), f32)] * 2
                        + [pltpu.VMEM((_padded(ATT_SUPER), ATT_HEAD_DIM), f32)]
                        + [pltpu.VMEM((ATT_SUPER, ATT_HEAD_DIM), f32)]
                        + [pltpu.VMEM((_padded(ATT_SUPER), ATT_HEAD_DIM), f32)] * 3
                        + [pltpu.VMEM((ATT_SUPER, ATT_HEAD_DIM), f32)] * 3),
        compiler_params=_params(("parallel", "parallel", "arbitrary"), 40),
        name="mix_attention",
    )(qkv, qkv, qkv, bias)


def _merge_kernel(ret_ref, sg_ref, att_ref, gate_ref, wr_ref, ws_ref, wa_ref, wo_ref, x_ref, g_ref,
                  xo_ref, ho_ref):
    d = D_MODEL
    merged = gate_ref[:, 0:d].astype(f32) * jnp.dot(ret_ref[...], wr_ref[...], preferred_element_type=f32)
    merged += gate_ref[:, d:2 * d].astype(f32) * jnp.dot(sg_ref[...], ws_ref[...], preferred_element_type=f32)
    merged += gate_ref[:, 2 * d:3 * d].astype(f32) * jnp.dot(att_ref[...], wa_ref[...], preferred_element_type=f32)
    xn = x_ref[...] + jnp.dot(merged.astype(bf16), wo_ref[...], preferred_element_type=f32)
    xo_ref[...] = xn
    ho_ref[...] = _rms(xn, g_ref[...]).astype(ho_ref.dtype)


def merge_project(ret, sg, att, gates, w_ret, w_sg, w_att, w_out, layer, x, g_next, tm=256):
    n, d = x.shape

    def resident(w):
        return pl.BlockSpec((None,) + w.shape[1:], lambda i: (layer, 0, 0), pipeline_mode=pl.Buffered(1))

    def rows(w):
        return pl.BlockSpec((tm, w), lambda i: (i, 0))

    return pl.pallas_call(
        _merge_kernel,
        grid=(n // tm,),
        in_specs=[rows(RET_V_W), rows(SG_W), rows(ATT_W), rows(GATE_W),
                  resident(w_ret), resident(w_sg), resident(w_att), resident(w_out),
                  rows(d), pl.BlockSpec((1, d), lambda i: (0, 0))],
        out_specs=[rows(d), rows(d)],
        out_shape=[jax.ShapeDtypeStruct((n, d), f32), jax.ShapeDtypeStruct((n, d), bf16)],
        compiler_params=_params(("parallel",), 60),
        name="mix_merge",
    )(ret, sg, att, gates, w_ret, w_sg, w_att, w_out, x, g_next)


def kernel(x, ffn1_norm, ffn1_w_gate, ffn1_w_up, ffn1_w_down, mix_norm, w_in, b_gate, sg_ln_g, sg_ln_b,
           sg_w, sg_b, rel_bias, w_proj_ret, w_proj_sg, w_proj_att, w_out, ffn2_norm, ffn2_w_gate,
           ffn2_w_up, ffn2_w_down, final_norm):
    b, s, d = x.shape
    n = b * s
    depth = ffn1_norm.shape[0]

    def w16(w):
        return w.astype(bf16)

    ffn1_wd, ffn2_wd = w16(ffn1_w_down), w16(ffn2_w_down)
    w_ret16, w_sg16, w_att16, w_out16 = w16(w_proj_ret), w16(w_proj_sg), w16(w_proj_att), w16(w_out)
    b_gate3 = b_gate[:, None, :]
    ln_g3, ln_b3 = sg_ln_g[:, None, :], sg_ln_b[:, None, :]
    sg_b4 = sg_b[..., None]
    tables = _retention_tables(s)
    cos, sin = _rotary_tables(s)
    att_bias = _attention_bias(rel_bias)

    def seq3(t):
        return t.reshape(b, s, t.shape[-1])

    xr = x.reshape(n, d)
    h = rmsnorm(xr, ffn1_norm[0][None], bf16)
    for l in range(depth):
        a = gateup(h, ffn1_w_gate, ffn1_w_up, l)
        xr, h = down_residual_norm(a, ffn1_wd, l, xr, mix_norm[l][None], bf16)

        qk, rv, rg, sg_in, qkv, gates = mixer_inputs(h, w_in, b_gate3, l, cos, sin)
        ret = retention(seq3(qk), seq3(rv), seq3(rg), tables).reshape(n, RET_V_W)
        sg = spatial_gating(sg_in, ln_g3, ln_b3, sg_w, sg_b4, l)
        att = dilated_attention(seq3(qkv), att_bias).reshape(n, ATT_W)
        xr, h = merge_project(ret, sg, att, gates, w_ret16, w_sg16, w_att16, w_out16, l, xr,
                              ffn2_norm[l][None])

        a = gateup(h, ffn2_w_gate, ffn2_w_up, l)
        last = l == depth - 1
        g_next = final_norm[None] if last else ffn1_norm[l + 1][None]
        xr, h = down_residual_norm(a, ffn2_wd, l, xr, g_next, f32 if last else bf16)
    return h.reshape(b, s, d)
```

```python
import functools
import math

import jax
import jax.numpy as jnp
from jax import lax
from jax.experimental import pallas as pl
from jax.experimental.pallas import tpu as pltpu

f32 = jnp.float32
bf16 = jnp.bfloat16

D_MODEL = 2048
DEPTH = 4
BLOCK = 128
RET_HEADS = 4
RET_QK_DIM = 256
RET_V_DIM = 512
SG_GROUPS = 4
SG_GROUP_DIM = 256
ATT_HEADS = 8
ATT_HEAD_DIM = 128
DILATION_CONFIGS = ((128, 1), (512, 4), (2048, 16))
REL_BUCKETS = 32
REL_MAX_DIST = 2048
D_FF = 5632
ROPE_BASE = 10000.0
EPS = 1e-6
NEG_INF = -1e30

RET_QK_W = RET_HEADS * RET_QK_DIM
RET_V_W = RET_HEADS * RET_V_DIM
SG_W = SG_GROUPS * SG_GROUP_DIM
ATT_W = ATT_HEADS * ATT_HEAD_DIM
N_BRANCH = 3
Z_W = 2 * RET_QK_W + 2 * RET_V_W + 2 * SG_W + 3 * ATT_W
GATE_W = N_BRANCH * D_MODEL

ATT_SUPER = 2048
ATT_GROUP = 4
ATT_GROUP_FIRST = 8
ATT_PAD_EVERY = 4
GATE_SLAB = 256
MM_ROW_CHUNK = 256
V7X_VMEM_LIMIT = 60 * 1024 * 1024


def _params(sem, vmem_mb=None):
    return pltpu.CompilerParams(
        dimension_semantics=sem,
        vmem_limit_bytes=None if vmem_mb is None else min(vmem_mb * 1024 * 1024, V7X_VMEM_LIMIT))


def _rms(x, g):
    return x * lax.rsqrt(jnp.mean(x * x, axis=-1, keepdims=True) + EPS) * g


def _rmsnorm_kernel(x_ref, g_ref, o_ref):
    o_ref[...] = _rms(x_ref[...], g_ref[...]).astype(o_ref.dtype)


def rmsnorm(x, g, out_dtype, tm=512):
    n, d = x.shape
    return pl.pallas_call(
        _rmsnorm_kernel,
        grid=(n // tm,),
        in_specs=[pl.BlockSpec((tm, d), lambda i: (i, 0)),
                  pl.BlockSpec((1, d), lambda i: (0, 0))],
        out_specs=pl.BlockSpec((tm, d), lambda i: (i, 0)),
        out_shape=jax.ShapeDtypeStruct((n, d), out_dtype),
        compiler_params=_params(("parallel",)),
        name="rmsnorm",
    )(x, g)


def _row_chunks(rows):
    return [pl.ds(c * MM_ROW_CHUNK, MM_ROW_CHUNK) for c in range(rows // MM_ROW_CHUNK)]


def _gateup_kernel(h_ref, wg_ref, wu_ref, o_ref, wg16_ref, wu16_ref):
    @pl.when(pl.program_id(1) == 0)
    def _():
        wg16_ref[...] = wg_ref[...].astype(bf16)
        wu16_ref[...] = wu_ref[...].astype(bf16)

    for rows in _row_chunks(h_ref.shape[0]):
        h = h_ref[rows, :]
        g = jnp.dot(h, wg16_ref[...], preferred_element_type=f32)
        u = jnp.dot(h, wu16_ref[...], preferred_element_type=f32)
        o_ref[rows, :] = (g * jax.nn.sigmoid(g) * u).astype(o_ref.dtype)


def gateup(h, wg, wu, layer, tm=2048, tf=512):
    n, d = h.shape
    dff = wg.shape[-1]
    return pl.pallas_call(
        _gateup_kernel,
        grid=(dff // tf, n // tm),
        in_specs=[pl.BlockSpec((tm, d), lambda j, i: (i, 0)),
                  pl.BlockSpec((None, d, tf), lambda j, i: (layer, 0, j)),
                  pl.BlockSpec((None, d, tf), lambda j, i: (layer, 0, j))],
        out_specs=pl.BlockSpec((tm, tf), lambda j, i: (i, j)),
        out_shape=jax.ShapeDtypeStruct((n, dff), bf16),
        scratch_shapes=[pltpu.VMEM((d, tf), bf16), pltpu.VMEM((d, tf), bf16)],
        compiler_params=_params(("arbitrary", "arbitrary"), 56),
        name="ffn_gateup",
    )(h, wg, wu)


def _down_kernel(a_ref, w_ref, x_ref, g_ref, xo_ref, ho_ref):
    acc = jnp.dot(a_ref[...], w_ref[...], preferred_element_type=f32)
    xn = x_ref[...] + 0.5 * acc
    xo_ref[...] = xn
    ho_ref[...] = _rms(xn, g_ref[...]).astype(ho_ref.dtype)


def down_residual_norm(a, wd, layer, x, g_next, h_dtype, tm=256):
    n, dff = a.shape
    d = wd.shape[-1]
    return pl.pallas_call(
        _down_kernel,
        grid=(n // tm,),
        in_specs=[pl.BlockSpec((tm, dff), lambda i: (i, 0)),
                  pl.BlockSpec((None, dff, d), lambda i: (layer, 0, 0), pipeline_mode=pl.Buffered(1)),
                  pl.BlockSpec((tm, d), lambda i: (i, 0)),
                  pl.BlockSpec((1, d), lambda i: (0, 0))],
        out_specs=[pl.BlockSpec((tm, d), lambda i: (i, 0)),
                   pl.BlockSpec((tm, d), lambda i: (i, 0))],
        out_shape=[jax.ShapeDtypeStruct((n, d), f32),
                   jax.ShapeDtypeStruct((n, d), h_dtype)],
        compiler_params=_params(("parallel",), 56),
        name="ffn_down",
    )(a, wd, x, g_next)


def _stream_project(h_ref, w16_ref, o_ref, epilogue):
    for rows in _row_chunks(h_ref.shape[0]):
        acc = jnp.dot(h_ref[rows, :], w16_ref[...], preferred_element_type=f32)
        o_ref[rows, :] = epilogue(rows, acc).astype(o_ref.dtype)


def _inproj_kernel(h_ref, w_ref, cos_ref, sin_ref, o_ref, w16_ref, *, tn):
    j = pl.program_id(0)

    @pl.when(pl.program_id(1) == 0)
    def _():
        w16_ref[...] = w_ref[...].astype(bf16)

    q_tiles = RET_QK_W // tn
    rot_tiles = 2 * q_tiles
    gate_lo = (2 * RET_QK_W + RET_V_W) // tn
    gate_hi = gate_lo + RET_V_W // tn
    half = RET_QK_DIM // 2

    @pl.when(j < rot_tiles)
    def _():
        kscale = jnp.where(j >= q_tiles, RET_QK_DIM ** -0.5, 1.0).astype(f32)

        def rotate(rows, acc):
            cos = cos_ref[rows, :] * kscale
            sin = sin_ref[rows, :] * kscale
            out = []
            for hd in range(tn // RET_QK_DIM):
                t1 = acc[:, hd * RET_QK_DIM:hd * RET_QK_DIM + half]
                t2 = acc[:, hd * RET_QK_DIM + half:(hd + 1) * RET_QK_DIM]
                out += [t1 * cos - t2 * sin, t1 * sin + t2 * cos]
            return jnp.concatenate(out, axis=-1)

        _stream_project(h_ref, w16_ref, o_ref, rotate)

    @pl.when((j >= gate_lo) & (j < gate_hi))
    def _():
        _stream_project(h_ref, w16_ref, o_ref, lambda rows, acc: acc * jax.nn.sigmoid(acc))

    @pl.when(((j >= rot_tiles) & (j < gate_lo)) | (j >= gate_hi))
    def _():
        _stream_project(h_ref, w16_ref, o_ref, lambda rows, acc: acc)


def in_projection(h, w_in, layer, cos, sin, tm=2048, tn=1024):
    n, d = h.shape
    seq_tiles = cos.shape[0] // tm
    table = pl.BlockSpec((tm, RET_QK_DIM // 2), lambda j, i: (i % seq_tiles, 0))
    return pl.pallas_call(
        functools.partial(_inproj_kernel, tn=tn),
        grid=(Z_W // tn, n // tm),
        in_specs=[pl.BlockSpec((tm, d), lambda j, i: (i, 0)),
                  pl.BlockSpec((None, d, tn), lambda j, i: (layer, 0, j)),
                  table, table],
        out_specs=pl.BlockSpec((tm, tn), lambda j, i: (i, j)),
        out_shape=jax.ShapeDtypeStruct((n, Z_W), bf16),
        scratch_shapes=[pltpu.VMEM((d, tn), bf16)],
        compiler_params=_params(("arbitrary", "arbitrary"), 58),
        name="mix_inproj",
    )(h, w_in, cos, sin)


def _retention_kernel(q_ref, k_ref, v_ref, g_ref, inner_ref, xi_ref, zeta_ref, gch_ref, o_ref, state_ref,
                      *, nchunk):
    @pl.when(pl.program_id(2) == 0)
    def _():
        state_ref[...] = jnp.zeros_like(state_ref)

    inner = inner_ref[...]
    xi = xi_ref[...]
    zeta = zeta_ref[...]
    gch = gch_ref[...]
    chunks = range(nchunk)
    rows = [pl.ds(c * BLOCK, BLOCK) for c in chunks]

    q = [q_ref[rows[c], :] for c in chunks]
    k = [k_ref[rows[c], :] for c in chunks]
    v = [v_ref[rows[c], :] for c in chunks]
    a = [lax.dot_general(q[c], k[c], (((1,), (1,)), ((), ())), preferred_element_type=f32) * inner
         for c in chunks]
    upd = [jnp.dot((k[c].astype(f32) * zeta).T.astype(bf16), v[c], preferred_element_type=f32)
           for c in chunks]
    local = [jnp.dot(a[c].astype(bf16), v[c], preferred_element_type=f32) for c in chunks]
    state = state_ref[...]
    carried = []
    for c in chunks:
        carried.append(jnp.dot(q[c], state.astype(bf16), preferred_element_type=f32))
        state = state * gch + upd[c]
    state_ref[...] = state
    for c in chunks:
        o = local[c] + carried[c] * xi
        on = o * lax.rsqrt(jnp.mean(o * o, axis=-1, keepdims=True) + EPS)
        o_ref[rows[c], :] = (g_ref[rows[c], :].astype(f32) * on).astype(o_ref.dtype)


def _retention_tables():
    h = jnp.arange(RET_HEADS, dtype=f32)
    log_g = jnp.log1p(-(2.0 ** (-5.0 - h)))
    idx = jnp.arange(BLOCK, dtype=f32)
    rel = idx[:, None] - idx[None, :]
    inner = jnp.where(rel[None] >= 0, jnp.exp(jnp.maximum(rel, 0.0)[None] * log_g[:, None, None]), 0.0)
    xi = jnp.exp((idx + 1.0)[None, :] * log_g[:, None])[..., None]
    zeta = jnp.exp((BLOCK - 1.0 - idx)[None, :] * log_g[:, None])[..., None]
    gch = jnp.broadcast_to(jnp.exp(BLOCK * log_g)[:, None, None], (RET_HEADS, 1, RET_V_DIM))
    return inner, xi, zeta, gch


def _rotary_tables(seq):
    inv = ROPE_BASE ** (-jnp.arange(0, RET_QK_DIM, 2, dtype=f32) / RET_QK_DIM)
    ang = jnp.arange(seq).astype(f32)[:, None] * inv[None, :]
    return jnp.cos(ang), jnp.sin(ang)


def retention(z, tables, tokens=1024):
    b, s, _ = z.shape
    inner, xi, zeta, gch = tables
    nv = 2 * RET_QK_W // RET_V_DIM
    kern = functools.partial(_retention_kernel, nchunk=tokens // BLOCK)
    return pl.pallas_call(
        kern,
        grid=(b, RET_HEADS, s // tokens),
        in_specs=[pl.BlockSpec((None, tokens, RET_QK_DIM), lambda bi, h, t: (bi, t, h)),
                  pl.BlockSpec((None, tokens, RET_QK_DIM), lambda bi, h, t: (bi, t, RET_HEADS + h)),
                  pl.BlockSpec((None, tokens, RET_V_DIM), lambda bi, h, t: (bi, t, nv + h)),
                  pl.BlockSpec((None, tokens, RET_V_DIM), lambda bi, h, t: (bi, t, nv + RET_HEADS + h)),
                  pl.BlockSpec((None, BLOCK, BLOCK), lambda bi, h, t: (h, 0, 0)),
                  pl.BlockSpec((None, BLOCK, 1), lambda bi, h, t: (h, 0, 0)),
                  pl.BlockSpec((None, BLOCK, 1), lambda bi, h, t: (h, 0, 0)),
                  pl.BlockSpec((None, 1, RET_V_DIM), lambda bi, h, t: (h, 0, 0))],
        out_specs=pl.BlockSpec((None, tokens, RET_V_DIM), lambda bi, h, t: (bi, t, h)),
        out_shape=jax.ShapeDtypeStruct((b, s, RET_V_W), bf16),
        scratch_shapes=[pltpu.VMEM((RET_QK_DIM, RET_V_DIM), f32)],
        compiler_params=_params(("parallel", "parallel", "arbitrary"), 32),
        name="mix_retention",
    )(z, z, z, z, inner, xi, zeta, gch)


def _sgu_kernel(u_ref, v_ref, lg_ref, lb_ref, w_ref, b_ref, o_ref, *, nchunk):
    u = jax.nn.gelu(u_ref[...].astype(f32))
    v = jax.nn.gelu(v_ref[...].astype(f32))
    mu = jnp.mean(v, axis=-1, keepdims=True)
    var = jnp.mean(jnp.square(v - mu), axis=-1, keepdims=True)
    vn = ((v - mu) * lax.rsqrt(var + EPS) * lg_ref[...] + lb_ref[...]).astype(bf16)
    row = lax.broadcasted_iota(jnp.int32, (BLOCK, BLOCK), 0)
    col = lax.broadcasted_iota(jnp.int32, (BLOCK, BLOCK), 1)
    causal = row >= col
    for g in range(SG_GROUPS):
        wc = jnp.where(causal, w_ref[g], 0.0).astype(bf16)
        bias = b_ref[g]
        cols = slice(g * SG_GROUP_DIM, (g + 1) * SG_GROUP_DIM)
        for c in range(nchunk):
            rows = slice(c * BLOCK, (c + 1) * BLOCK)
            mix = jnp.dot(wc, vn[rows, cols], preferred_element_type=f32) + bias
            o_ref[rows, cols] = (u[rows, cols] * mix).astype(o_ref.dtype)


def spatial_gating(z2, ln_g, ln_b, sg_w, sg_b, layer, tokens=512):
    n = z2.shape[0]
    off = (2 * RET_QK_W + 2 * RET_V_W) // SG_W
    kern = functools.partial(_sgu_kernel, nchunk=tokens // BLOCK)
    return pl.pallas_call(
        kern,
        grid=(n // tokens,),
        in_specs=[pl.BlockSpec((tokens, SG_W), lambda i: (i, off)),
                  pl.BlockSpec((tokens, SG_W), lambda i: (i, off + 1)),
                  pl.BlockSpec((None, 1, SG_W), lambda i: (layer, 0, 0)),
                  pl.BlockSpec((None, 1, SG_W), lambda i: (layer, 0, 0)),
                  pl.BlockSpec((None, SG_GROUPS, BLOCK, BLOCK), lambda i: (layer, 0, 0, 0)),
                  pl.BlockSpec((None, SG_GROUPS, BLOCK, 1), lambda i: (layer, 0, 0, 0))],
        out_specs=pl.BlockSpec((tokens, SG_W), lambda i: (i, 0)),
        out_shape=jax.ShapeDtypeStruct((n, SG_W), bf16),
        compiler_params=_params(("parallel",), 32),
        name="mix_sgu",
    )(z2, z2, ln_g, ln_b, sg_w, sg_b)


def _padded(t):
    return t + t // ATT_PAD_EVERY


def _attention_gates_kernel(q_ref, k_ref, v_ref, bias_ref, h_ref, w0_ref, w1_ref, w2_ref, b_ref,
                            o_ref, gate_ref, kp_ref, vp_ref, kd_ref, vd_ref, qp_ref, stage_ref,
                            mp_ref, lp_ref, accp_ref, mn_ref, ln_ref, accn_ref, w16_ref):
    sb = pl.program_id(2)
    scale = ATT_HEAD_DIM ** -0.5
    quarter = ATT_SUPER // ATT_PAD_EVERY
    super_p = _padded(ATT_SUPER)
    cur_base = lax.rem(sb, 2) * super_p
    prev_base = super_p - cur_base

    def pad_rows(dst_ref, dst_base):
        for j in range(ATT_PAD_EVERY):
            dst_ref[pl.ds(dst_base + j, quarter, stride=ATT_PAD_EVERY + 1), :] = (
                stage_ref[pl.ds(j, quarter, stride=ATT_PAD_EVERY), :])

    @pl.when(sb == 0)
    def _():
        for t, w_ref in enumerate((w0_ref, w1_ref, w2_ref)):
            w16_ref[:, t * GATE_SLAB:(t + 1) * GATE_SLAB] = w_ref[...].astype(bf16)
        kd_ref[pl.ds(0, BLOCK), :] = k_ref[pl.ds(0, BLOCK), :]
        vd_ref[pl.ds(0, BLOCK), :] = v_ref[pl.ds(0, BLOCK), :]

    @pl.when(sb > 0)
    def _():
        kd_ref[pl.ds(0, BLOCK), :] = kd_ref[pl.ds(ATT_SUPER, BLOCK), :]
        vd_ref[pl.ds(0, BLOCK), :] = vd_ref[pl.ds(ATT_SUPER, BLOCK), :]

    kd_ref[pl.ds(BLOCK, ATT_SUPER), :] = k_ref[...]
    vd_ref[pl.ds(BLOCK, ATT_SUPER), :] = v_ref[...]
    for src_ref, dst_ref, base in ((k_ref, kp_ref, cur_base), (v_ref, vp_ref, cur_base), (q_ref, qp_ref, 0)):
        stage_ref[...] = src_ref[...].astype(f32)
        pad_rows(dst_ref, base)

    def gate_chunk(c):
        rows = pl.ds(pl.multiple_of(c * MM_ROW_CHUNK, MM_ROW_CHUNK), MM_ROW_CHUNK)
        acc = jnp.dot(h_ref[rows, :], w16_ref[...], preferred_element_type=f32)
        gate_ref[rows, :] = jax.nn.sigmoid(acc + b_ref[...]).astype(gate_ref.dtype)

    def softmax(s):
        mb = jnp.max(s, axis=-1, keepdims=True)
        p = jnp.exp(s - mb)
        return mb, jnp.sum(p, axis=-1, keepdims=True), p.astype(bf16)

    def merged(state, mb, lb, ob):
        mb = jnp.broadcast_to(mb, (BLOCK, ATT_HEAD_DIM))
        lb = jnp.broadcast_to(lb, (BLOCK, ATT_HEAD_DIM))
        if state is None:
            return mb, lb, ob
        m_old, l_old, acc_old = state
        m_new = jnp.maximum(m_old, mb)
        wa = jnp.exp(m_old - m_new)
        wb = jnp.exp(mb - m_new)
        return m_new, wa * l_old + wb * lb, wa * acc_old + wb * ob

    def run_pass(nblocks, group, load, finish, first_gate_chunk=None):
        def body(step, carry):
            its = [step * group + g for g in range(group)]
            ins = [load(it) for it in its]
            logit = [lax.dot_general(q, k, (((1,), (1,)), ((), ())), preferred_element_type=f32) * scale + b
                     for q, k, _, b, _ in ins]
            if first_gate_chunk is not None:
                gate_chunk(first_gate_chunk + step)
            soft = [softmax(s) for s in logit]
            outs = [jnp.dot(soft[g][2], ins[g][2], preferred_element_type=f32) for g in range(group)]
            new = [merged(ins[g][4], soft[g][0], soft[g][1], outs[g]) for g in range(group)]
            for g in range(group):
                finish(its[g], *new[g])
            return carry

        lax.fori_loop(0, nblocks // group, body, 0)

    dense_groups = ATT_SUPER // BLOCK // ATT_GROUP
    gate_chunks = h_ref.shape[0] // MM_ROW_CHUNK

    for bi, (window, d) in reversed(list(enumerate(DILATION_CONFIGS))[1:]):
        assert window // d == BLOCK and ATT_SUPER % (d * BLOCK) == 0 and d % ATT_PAD_EVERY == 0
        span_p = _padded(d * BLOCK)
        blocks_per_super = ATT_SUPER // (d * BLOCK)
        shift = d.bit_length() - 1
        stride = _padded(d)
        is_first_pass = bi == len(DILATION_CONFIGS) - 1

        def rows(start, stride=stride):
            return pl.ds(start, BLOCK, stride=stride)

        def load(it, bi=bi, d=d, span_p=span_p, blocks_per_super=blocks_per_super, shift=shift,
                 rows=rows, is_first_pass=is_first_pass):
            n_loc = lax.shift_right_logical(it, shift)
            r = lax.bitwise_and(it, d - 1)
            r_p = r + lax.shift_right_logical(r, 2)
            first = jnp.logical_and(sb == 0, n_loc == 0)
            q_rows = rows(n_loc * span_p + r_p)
            cur = cur_base + n_loc * span_p + r_p
            before = jnp.where(sb == 0, cur_base, prev_base + (blocks_per_super - 1) * span_p)
            prev = jnp.where(n_loc > 0, cur_base + (n_loc - 1) * span_p, before) + r_p
            k = jnp.concatenate([kp_ref[rows(prev), :], kp_ref[rows(cur), :]], axis=0).astype(bf16)
            v = jnp.concatenate([vp_ref[rows(prev), :], vp_ref[rows(cur), :]], axis=0).astype(bf16)
            state = None if is_first_pass else (mp_ref[q_rows, :], lp_ref[q_rows, :], accp_ref[q_rows, :])
            return qp_ref[q_rows, :].astype(bf16), k, v, bias_ref[bi, first.astype(jnp.int32)], state

        def finish(it, m, l, acc, bi=bi, d=d, span_p=span_p, shift=shift, rows=rows):
            n_loc = lax.shift_right_logical(it, shift)
            r = lax.bitwise_and(it, d - 1)
            if bi == 1:
                q_rows = pl.ds(n_loc * (d * BLOCK) + r, BLOCK, stride=d)
                mn_ref[q_rows, :], ln_ref[q_rows, :], accn_ref[q_rows, :] = m, l, acc
            else:
                q_rows = rows(n_loc * span_p + r + lax.shift_right_logical(r, 2))
                mp_ref[q_rows, :], lp_ref[q_rows, :], accp_ref[q_rows, :] = m, l, acc

        if is_first_pass:
            run_pass(blocks_per_super * d, ATT_GROUP_FIRST, load, finish)
        else:
            assert blocks_per_super * d // ATT_GROUP + dense_groups == gate_chunks
            run_pass(blocks_per_super * d, ATT_GROUP, load, finish, first_gate_chunk=0)

    window, d = DILATION_CONFIGS[0]
    assert d == 1 and window == BLOCK and len(DILATION_CONFIGS) == 3

    def load_dense(it):
        q_rows = pl.ds(pl.multiple_of(it * BLOCK, BLOCK), BLOCK)
        kv_rows = pl.ds(pl.multiple_of(it * BLOCK, BLOCK), 2 * BLOCK)
        first = jnp.logical_and(sb == 0, it == 0).astype(jnp.int32)
        return (q_ref[q_rows, :], kd_ref[kv_rows, :], vd_ref[kv_rows, :], bias_ref[0, first],
                (mn_ref[q_rows, :], ln_ref[q_rows, :], accn_ref[q_rows, :]))

    def finish_dense(it, m, l, acc):
        o_ref[pl.ds(pl.multiple_of(it * BLOCK, BLOCK), BLOCK), :] = (acc / l).astype(o_ref.dtype)

    run_pass(ATT_SUPER // BLOCK, ATT_GROUP, load_dense, finish_dense, first_gate_chunk=gate_chunks - dense_groups)


def _t5_bucket(dist):
    max_exact = REL_BUCKETS // 2
    d_f = jnp.maximum(dist, 1).astype(f32)
    large = max_exact + (jnp.log(d_f / max_exact) / math.log(REL_MAX_DIST / max_exact)
                         * (REL_BUCKETS - max_exact)).astype(jnp.int32)
    large = jnp.minimum(large, REL_BUCKETS - 1)
    return jnp.where(dist < max_exact, dist, large)


def _attention_bias(rel_bias):
    i = jnp.arange(BLOCK)[:, None]
    j = jnp.arange(2 * BLOCK)[None, :]
    steps = BLOCK + i - j
    band = (steps >= 0) & (steps <= BLOCK)
    tab = rel_bias.astype(f32)
    out = []
    for _, d in DILATION_CONFIGS:
        onehot = jax.nn.one_hot(_t5_bucket(d * jnp.maximum(steps, 0)), REL_BUCKETS, dtype=f32)
        bias = jnp.einsum('qkb,bh->hqk', onehot, tab, precision=lax.Precision.HIGHEST)
        later = jnp.where(band[None], bias, NEG_INF)
        first = jnp.where((band & (j >= BLOCK))[None], bias, NEG_INF)
        out.append(jnp.stack([later, first], axis=1))
    return jnp.stack(out)


def attention_and_gates(z, bias, h, w_in, b_gate, layer):
    b, s, _ = z.shape
    d = h.shape[1]
    off = (2 * RET_QK_W + 2 * RET_V_W + 2 * SG_W) // ATT_HEAD_DIM
    supers = s // ATT_SUPER
    per_head = GATE_W // ATT_HEADS
    slabs = per_head // GATE_SLAB
    assert slabs == 3 and per_head % GATE_SLAB == 0 and Z_W % GATE_SLAB == 0

    def tile(col):
        return pl.BlockSpec((None, ATT_SUPER, ATT_HEAD_DIM), lambda bi, hd, t: (bi, t, col + hd))

    def weight(t):
        return pl.BlockSpec((None, d, GATE_SLAB), lambda bi, hd, _: (layer, 0, Z_W // GATE_SLAB + hd * slabs + t),
                            pipeline_mode=pl.Buffered(1))

    return pl.pallas_call(
        _attention_gates_kernel,
        grid=(b, ATT_HEADS, supers),
        in_specs=[tile(off), tile(off + ATT_HEADS), tile(off + 2 * ATT_HEADS),
                  pl.BlockSpec((N_BRANCH, None, 2, BLOCK, 2 * BLOCK), lambda bi, hd, t: (0, hd, 0, 0, 0)),
                  pl.BlockSpec((ATT_SUPER, d), lambda bi, hd, t: (bi * supers + t, 0)),
                  weight(0), weight(1), weight(2),
                  pl.BlockSpec((None, 1, per_head), lambda bi, hd, t: (layer, 0, hd))],
        out_specs=[pl.BlockSpec((None, ATT_SUPER, ATT_HEAD_DIM), lambda bi, hd, t: (bi, t, hd)),
                   pl.BlockSpec((ATT_SUPER, per_head), lambda bi, hd, t: (bi * supers + t, hd))],
        out_shape=[jax.ShapeDtypeStruct((b, s, ATT_W), bf16),
                   jax.ShapeDtypeStruct((b * s, GATE_W), bf16)],
        scratch_shapes=([pltpu.VMEM((2 * _padded(ATT_SUPER), ATT_HEAD_DIM), f32)] * 2
                        + [pltpu.VMEM((BLOCK + ATT_SUPER, ATT_HEAD_DIM), bf16)] * 2
                        + [pltpu.VMEM((_padded(ATT_SUPER), ATT_HEAD_DIM), f32)]
                        + [pltpu.VMEM((ATT_SUPER, ATT_HEAD_DIM), f32)]
                        + [pltpu.VMEM((_padded(ATT_SUPER), ATT_HEAD_DIM), f32)] * 3
                        + [pltpu.VMEM((ATT_SUPER, ATT_HEAD_DIM), f32)] * 3
                        + [pltpu.VMEM((d, per_head), bf16)]),
        compiler_params=_params(("parallel", "parallel", "arbitrary"), 60),
        name="mix_attention_gates",
    )(z, z, z, bias, h, w_in, w_in, w_in, b_gate)


def _merge_kernel(ret_ref, sg_ref, att_ref, gate_ref, wr_ref, ws_ref, wa_ref, wo_ref, x_ref, g_ref,
                  xo_ref, ho_ref):
    d = D_MODEL
    merged = gate_ref[:, 0:d].astype(f32) * jnp.dot(ret_ref[...], wr_ref[...], preferred_element_type=f32)
    merged += gate_ref[:, d:2 * d].astype(f32) * jnp.dot(sg_ref[...], ws_ref[...], preferred_element_type=f32)
    merged += gate_ref[:, 2 * d:3 * d].astype(f32) * jnp.dot(att_ref[...], wa_ref[...], preferred_element_type=f32)
    xn = x_ref[...] + jnp.dot(merged.astype(bf16), wo_ref[...], preferred_element_type=f32)
    xo_ref[...] = xn
    ho_ref[...] = _rms(xn, g_ref[...]).astype(ho_ref.dtype)


def merge_project(ret, sg, att, gates, w_ret, w_sg, w_att, w_out, layer, x, g_next, tm=256):
    n, d = x.shape

    def resident(w):
        return pl.BlockSpec((None,) + w.shape[1:], lambda i: (layer, 0, 0), pipeline_mode=pl.Buffered(1))

    def rows(w):
        return pl.BlockSpec((tm, w), lambda i: (i, 0))

    return pl.pallas_call(
        _merge_kernel,
        grid=(n // tm,),
        in_specs=[rows(RET_V_W), rows(SG_W), rows(ATT_W), rows(GATE_W),
                  resident(w_ret), resident(w_sg), resident(w_att), resident(w_out),
                  rows(d), pl.BlockSpec((1, d), lambda i: (0, 0))],
        out_specs=[rows(d), rows(d)],
        out_shape=[jax.ShapeDtypeStruct((n, d), f32), jax.ShapeDtypeStruct((n, d), bf16)],
        compiler_params=_params(("parallel",), 60),
        name="mix_merge",
    )(ret, sg, att, gates, w_ret, w_sg, w_att, w_out, x, g_next)


def kernel(x, ffn1_norm, ffn1_w_gate, ffn1_w_up, ffn1_w_down, mix_norm, w_in, b_gate, sg_ln_g, sg_ln_b,
           sg_w, sg_b, rel_bias, w_proj_ret, w_proj_sg, w_proj_att, w_out, ffn2_norm, ffn2_w_gate,
           ffn2_w_up, ffn2_w_down, final_norm):
    b, s, d = x.shape
    n = b * s
    depth = ffn1_norm.shape[0]

    def w16(w):
        return w.astype(bf16)

    ffn1_wd, ffn2_wd = w16(ffn1_w_down), w16(ffn2_w_down)
    w_ret16, w_sg16, w_att16, w_out16 = w16(w_proj_ret), w16(w_proj_sg), w16(w_proj_att), w16(w_out)
    b_gate3 = b_gate[:, None, :]
    ln_g3, ln_b3 = sg_ln_g[:, None, :], sg_ln_b[:, None, :]
    sg_b4 = sg_b[..., None]
    tables = _retention_tables()
    cos, sin = _rotary_tables(s)
    att_bias = _attention_bias(rel_bias)

    xr = x.reshape(n, d)
    h = rmsnorm(xr, ffn1_norm[0][None], bf16)
    for l in range(depth):
        a = gateup(h, ffn1_w_gate, ffn1_w_up, l)
        xr, h = down_residual_norm(a, ffn1_wd, l, xr, mix_norm[l][None], bf16)

        z = in_projection(h, w_in, l, cos, sin)
        z3 = z.reshape(b, s, Z_W)
        ret = retention(z3, tables).reshape(n, RET_V_W)
        sg = spatial_gating(z, ln_g3, ln_b3, sg_w, sg_b4, l)
        att, gates = attention_and_gates(z3, att_bias, h, w_in, b_gate3, l)
        att = att.reshape(n, ATT_W)
        xr, h = merge_project(ret, sg, att, gates, w_ret16, w_sg16, w_att16, w_out16, l, xr,
                              ffn2_norm[l][None])

        a = gateup(h, ffn2_w_gate, ffn2_w_up, l)
        last = l == depth - 1
        g_next = final_norm[None] if last else ffn1_norm[l + 1][None]
        xr, h = down_residual_norm(a, ffn2_wd, l, xr, g_next, f32 if last else bf16)
    return h.reshape(b, s, d)
```

```python
import functools
import math

import jax
import jax.numpy as jnp
from jax import lax
from jax.experimental import pallas as pl
from jax.experimental.pallas import tpu as pltpu

f32 = jnp.float32
bf16 = jnp.bfloat16

D_MODEL = 2048
DEPTH = 4
BLOCK = 128
RET_HEADS = 4
RET_QK_DIM = 256
RET_V_DIM = 512
SG_GROUPS = 4
SG_GROUP_DIM = 256
ATT_HEADS = 8
ATT_HEAD_DIM = 128
DILATION_CONFIGS = ((128, 1), (512, 4), (2048, 16))
REL_BUCKETS = 32
REL_MAX_DIST = 2048
D_FF = 5632
ROPE_BASE = 10000.0
EPS = 1e-6
NEG_INF = -1e30

RET_QK_W = RET_HEADS * RET_QK_DIM
RET_V_W = RET_HEADS * RET_V_DIM
SG_W = SG_GROUPS * SG_GROUP_DIM
ATT_W = ATT_HEADS * ATT_HEAD_DIM
N_BRANCH = 3
Z_W = 2 * RET_QK_W + 2 * RET_V_W + 2 * SG_W + 3 * ATT_W
GATE_W = N_BRANCH * D_MODEL

ATT_SUPER = 2048
ATT_GROUP = 4
ATT_GROUP_FIRST = 8
ATT_PAD_EVERY = 4
MM_ROW_CHUNK = 256
V7X_VMEM_LIMIT = 60 * 1024 * 1024


def _params(sem, vmem_mb=None):
    return pltpu.CompilerParams(
        dimension_semantics=sem,
        vmem_limit_bytes=None if vmem_mb is None else min(vmem_mb * 1024 * 1024, V7X_VMEM_LIMIT))


def _rms(x, g):
    return x * lax.rsqrt(jnp.mean(x * x, axis=-1, keepdims=True) + EPS) * g


def _rmsnorm_kernel(x_ref, g_ref, o_ref):
    o_ref[...] = _rms(x_ref[...], g_ref[...]).astype(o_ref.dtype)


def rmsnorm(x, g, out_dtype, tm=512):
    n, d = x.shape
    return pl.pallas_call(
        _rmsnorm_kernel,
        grid=(n // tm,),
        in_specs=[pl.BlockSpec((tm, d), lambda i: (i, 0)),
                  pl.BlockSpec((1, d), lambda i: (0, 0))],
        out_specs=pl.BlockSpec((tm, d), lambda i: (i, 0)),
        out_shape=jax.ShapeDtypeStruct((n, d), out_dtype),
        compiler_params=_params(("parallel",)),
        name="rmsnorm",
    )(x, g)


def _row_chunks(rows):
    return [pl.ds(c * MM_ROW_CHUNK, MM_ROW_CHUNK) for c in range(rows // MM_ROW_CHUNK)]


def _gateup_kernel(h_ref, wg_ref, wu_ref, o_ref, wg16_ref, wu16_ref):
    @pl.when(pl.program_id(1) == 0)
    def _():
        wg16_ref[...] = wg_ref[...].astype(bf16)
        wu16_ref[...] = wu_ref[...].astype(bf16)

    for rows in _row_chunks(h_ref.shape[0]):
        h = h_ref[rows, :]
        g = jnp.dot(h, wg16_ref[...], preferred_element_type=f32)
        u = jnp.dot(h, wu16_ref[...], preferred_element_type=f32)
        o_ref[rows, :] = (g * jax.nn.sigmoid(g) * u).astype(o_ref.dtype)


def gateup(h, wg, wu, layer, tm=2048, tf=512):
    n, d = h.shape
    dff = wg.shape[-1]
    return pl.pallas_call(
        _gateup_kernel,
        grid=(dff // tf, n // tm),
        in_specs=[pl.BlockSpec((tm, d), lambda j, i: (i, 0)),
                  pl.BlockSpec((None, d, tf), lambda j, i: (layer, 0, j)),
                  pl.BlockSpec((None, d, tf), lambda j, i: (layer, 0, j))],
        out_specs=pl.BlockSpec((tm, tf), lambda j, i: (i, j)),
        out_shape=jax.ShapeDtypeStruct((n, dff), bf16),
        scratch_shapes=[pltpu.VMEM((d, tf), bf16), pltpu.VMEM((d, tf), bf16)],
        compiler_params=_params(("arbitrary", "arbitrary"), 56),
        name="ffn_gateup",
    )(h, wg, wu)


def _down_kernel(a_ref, w_ref, x_ref, g_ref, xo_ref, ho_ref):
    acc = jnp.dot(a_ref[...], w_ref[...], preferred_element_type=f32)
    xn = x_ref[...] + 0.5 * acc
    xo_ref[...] = xn
    ho_ref[...] = _rms(xn, g_ref[...]).astype(ho_ref.dtype)


def down_residual_norm(a, wd, layer, x, g_next, h_dtype, tm=256):
    n, dff = a.shape
    d = wd.shape[-1]
    return pl.pallas_call(
        _down_kernel,
        grid=(n // tm,),
        in_specs=[pl.BlockSpec((tm, dff), lambda i: (i, 0)),
                  pl.BlockSpec((None, dff, d), lambda i: (layer, 0, 0), pipeline_mode=pl.Buffered(1)),
                  pl.BlockSpec((tm, d), lambda i: (i, 0)),
                  pl.BlockSpec((1, d), lambda i: (0, 0))],
        out_specs=[pl.BlockSpec((tm, d), lambda i: (i, 0)),
                   pl.BlockSpec((tm, d), lambda i: (i, 0))],
        out_shape=[jax.ShapeDtypeStruct((n, d), f32),
                   jax.ShapeDtypeStruct((n, d), h_dtype)],
        compiler_params=_params(("parallel",), 56),
        name="ffn_down",
    )(a, wd, x, g_next)


def _stream_project(h_ref, w16_ref, o_ref, epilogue):
    for rows in _row_chunks(h_ref.shape[0]):
        acc = jnp.dot(h_ref[rows, :], w16_ref[...], preferred_element_type=f32)
        o_ref[rows, :] = epilogue(rows, acc).astype(o_ref.dtype)


def _inproj_kernel(h_ref, w_ref, cos_ref, sin_ref, o_ref, w16_ref, *, tn):
    j = pl.program_id(0)

    @pl.when(pl.program_id(1) == 0)
    def _():
        w16_ref[...] = w_ref[...].astype(bf16)

    q_tiles = RET_QK_W // tn
    rot_tiles = 2 * q_tiles
    gate_lo = (2 * RET_QK_W + RET_V_W) // tn
    gate_hi = gate_lo + RET_V_W // tn
    half = RET_QK_DIM // 2

    @pl.when(j < rot_tiles)
    def _():
        kscale = jnp.where(j >= q_tiles, RET_QK_DIM ** -0.5, 1.0).astype(f32)

        def rotate(rows, acc):
            cos = cos_ref[rows, :] * kscale
            sin = sin_ref[rows, :] * kscale
            out = []
            for hd in range(tn // RET_QK_DIM):
                t1 = acc[:, hd * RET_QK_DIM:hd * RET_QK_DIM + half]
                t2 = acc[:, hd * RET_QK_DIM + half:(hd + 1) * RET_QK_DIM]
                out += [t1 * cos - t2 * sin, t1 * sin + t2 * cos]
            return jnp.concatenate(out, axis=-1)

        _stream_project(h_ref, w16_ref, o_ref, rotate)

    @pl.when((j >= gate_lo) & (j < gate_hi))
    def _():
        _stream_project(h_ref, w16_ref, o_ref, lambda rows, acc: acc * jax.nn.sigmoid(acc))

    @pl.when(((j >= rot_tiles) & (j < gate_lo)) | (j >= gate_hi))
    def _():
        _stream_project(h_ref, w16_ref, o_ref, lambda rows, acc: acc)


def _ingate_kernel(h_ref, w_ref, b_ref, o_ref, w16_ref):
    @pl.when(pl.program_id(1) == 0)
    def _():
        w16_ref[...] = w_ref[...].astype(bf16)

    _stream_project(h_ref, w16_ref, o_ref, lambda rows, acc: jax.nn.sigmoid(acc + b_ref[...]))


def in_projection(h, w_in, layer, cos, sin, tm=2048, tn=1024):
    n, d = h.shape
    seq_tiles = cos.shape[0] // tm
    table = pl.BlockSpec((tm, RET_QK_DIM // 2), lambda j, i: (i % seq_tiles, 0))
    return pl.pallas_call(
        functools.partial(_inproj_kernel, tn=tn),
        grid=(Z_W // tn, n // tm),
        in_specs=[pl.BlockSpec((tm, d), lambda j, i: (i, 0)),
                  pl.BlockSpec((None, d, tn), lambda j, i: (layer, 0, j)),
                  table, table],
        out_specs=pl.BlockSpec((tm, tn), lambda j, i: (i, j)),
        out_shape=jax.ShapeDtypeStruct((n, Z_W), bf16),
        scratch_shapes=[pltpu.VMEM((d, tn), bf16)],
        compiler_params=_params(("arbitrary", "arbitrary"), 58),
        name="mix_inproj",
    )(h, w_in, cos, sin)


def in_gates(h, w_in, b_gate, layer, tm=2048, tn=1024):
    n, d = h.shape
    off = Z_W // tn
    return pl.pallas_call(
        _ingate_kernel,
        grid=(GATE_W // tn, n // tm),
        in_specs=[pl.BlockSpec((tm, d), lambda j, i: (i, 0)),
                  pl.BlockSpec((None, d, tn), lambda j, i: (layer, 0, j + off)),
                  pl.BlockSpec((None, 1, tn), lambda j, i: (layer, 0, j))],
        out_specs=pl.BlockSpec((tm, tn), lambda j, i: (i, j)),
        out_shape=jax.ShapeDtypeStruct((n, GATE_W), bf16),
        scratch_shapes=[pltpu.VMEM((d, tn), bf16)],
        compiler_params=_params(("arbitrary", "arbitrary"), 58),
        name="mix_ingate",
    )(h, w_in, b_gate)


def _retention_kernel(q_ref, k_ref, v_ref, g_ref, inner_ref, xi_ref, zeta_ref, gch_ref, o_ref, state_ref,
                      *, nchunk):
    @pl.when(pl.program_id(2) == 0)
    def _():
        state_ref[...] = jnp.zeros_like(state_ref)

    inner = inner_ref[...]
    xi = xi_ref[...]
    zeta = zeta_ref[...]
    gch = gch_ref[...]
    chunks = range(nchunk)
    rows = [pl.ds(c * BLOCK, BLOCK) for c in chunks]

    q = [q_ref[rows[c], :] for c in chunks]
    k = [k_ref[rows[c], :] for c in chunks]
    v = [v_ref[rows[c], :] for c in chunks]
    a = [lax.dot_general(q[c], k[c], (((1,), (1,)), ((), ())), preferred_element_type=f32) * inner
         for c in chunks]
    upd = [jnp.dot((k[c].astype(f32) * zeta).T.astype(bf16), v[c], preferred_element_type=f32)
           for c in chunks]
    local = [jnp.dot(a[c].astype(bf16), v[c], preferred_element_type=f32) for c in chunks]
    state = state_ref[...]
    carried = []
    for c in chunks:
        carried.append(jnp.dot(q[c], state.astype(bf16), preferred_element_type=f32))
        state = state * gch + upd[c]
    state_ref[...] = state
    for c in chunks:
        o = local[c] + carried[c] * xi
        on = o * lax.rsqrt(jnp.mean(o * o, axis=-1, keepdims=True) + EPS)
        o_ref[rows[c], :] = (g_ref[rows[c], :].astype(f32) * on).astype(o_ref.dtype)


def _retention_tables():
    h = jnp.arange(RET_HEADS, dtype=f32)
    log_g = jnp.log1p(-(2.0 ** (-5.0 - h)))
    idx = jnp.arange(BLOCK, dtype=f32)
    rel = idx[:, None] - idx[None, :]
    inner = jnp.where(rel[None] >= 0, jnp.exp(jnp.maximum(rel, 0.0)[None] * log_g[:, None, None]), 0.0)
    xi = jnp.exp((idx + 1.0)[None, :] * log_g[:, None])[..., None]
    zeta = jnp.exp((BLOCK - 1.0 - idx)[None, :] * log_g[:, None])[..., None]
    gch = jnp.broadcast_to(jnp.exp(BLOCK * log_g)[:, None, None], (RET_HEADS, 1, RET_V_DIM))
    return inner, xi, zeta, gch


def _rotary_tables(seq):
    inv = ROPE_BASE ** (-jnp.arange(0, RET_QK_DIM, 2, dtype=f32) / RET_QK_DIM)
    ang = jnp.arange(seq).astype(f32)[:, None] * inv[None, :]
    return jnp.cos(ang), jnp.sin(ang)


def retention(z, tables, tokens=1024):
    b, s, _ = z.shape
    inner, xi, zeta, gch = tables
    nv = 2 * RET_QK_W // RET_V_DIM
    kern = functools.partial(_retention_kernel, nchunk=tokens // BLOCK)
    return pl.pallas_call(
        kern,
        grid=(b, RET_HEADS, s // tokens),
        in_specs=[pl.BlockSpec((None, tokens, RET_QK_DIM), lambda bi, h, t: (bi, t, h)),
                  pl.BlockSpec((None, tokens, RET_QK_DIM), lambda bi, h, t: (bi, t, RET_HEADS + h)),
                  pl.BlockSpec((None, tokens, RET_V_DIM), lambda bi, h, t: (bi, t, nv + h)),
                  pl.BlockSpec((None, tokens, RET_V_DIM), lambda bi, h, t: (bi, t, nv + RET_HEADS + h)),
                  pl.BlockSpec((None, BLOCK, BLOCK), lambda bi, h, t: (h, 0, 0)),
                  pl.BlockSpec((None, BLOCK, 1), lambda bi, h, t: (h, 0, 0)),
                  pl.BlockSpec((None, BLOCK, 1), lambda bi, h, t: (h, 0, 0)),
                  pl.BlockSpec((None, 1, RET_V_DIM), lambda bi, h, t: (h, 0, 0))],
        out_specs=pl.BlockSpec((None, tokens, RET_V_DIM), lambda bi, h, t: (bi, t, h)),
        out_shape=jax.ShapeDtypeStruct((b, s, RET_V_W), bf16),
        scratch_shapes=[pltpu.VMEM((RET_QK_DIM, RET_V_DIM), f32)],
        compiler_params=_params(("parallel", "parallel", "arbitrary"), 32),
        name="mix_retention",
    )(z, z, z, z, inner, xi, zeta, gch)


def _sgu_kernel(u_ref, v_ref, lg_ref, lb_ref, w_ref, b_ref, o_ref, *, nchunk):
    u = jax.nn.gelu(u_ref[...].astype(f32))
    v = jax.nn.gelu(v_ref[...].astype(f32))
    mu = jnp.mean(v, axis=-1, keepdims=True)
    var = jnp.mean(jnp.square(v - mu), axis=-1, keepdims=True)
    vn = ((v - mu) * lax.rsqrt(var + EPS) * lg_ref[...] + lb_ref[...]).astype(bf16)
    row = lax.broadcasted_iota(jnp.int32, (BLOCK, BLOCK), 0)
    col = lax.broadcasted_iota(jnp.int32, (BLOCK, BLOCK), 1)
    causal = row >= col
    for g in range(SG_GROUPS):
        wc = jnp.where(causal, w_ref[g], 0.0).astype(bf16)
        bias = b_ref[g]
        cols = slice(g * SG_GROUP_DIM, (g + 1) * SG_GROUP_DIM)
        for c in range(nchunk):
            rows = slice(c * BLOCK, (c + 1) * BLOCK)
            mix = jnp.dot(wc, vn[rows, cols], preferred_element_type=f32) + bias
            o_ref[rows, cols] = (u[rows, cols] * mix).astype(o_ref.dtype)


def spatial_gating(z2, ln_g, ln_b, sg_w, sg_b, layer, tokens=512):
    n = z2.shape[0]
    off = (2 * RET_QK_W + 2 * RET_V_W) // SG_W
    kern = functools.partial(_sgu_kernel, nchunk=tokens // BLOCK)
    return pl.pallas_call(
        kern,
        grid=(n // tokens,),
        in_specs=[pl.BlockSpec((tokens, SG_W), lambda i: (i, off)),
                  pl.BlockSpec((tokens, SG_W), lambda i: (i, off + 1)),
                  pl.BlockSpec((None, 1, SG_W), lambda i: (layer, 0, 0)),
                  pl.BlockSpec((None, 1, SG_W), lambda i: (layer, 0, 0)),
                  pl.BlockSpec((None, SG_GROUPS, BLOCK, BLOCK), lambda i: (layer, 0, 0, 0)),
                  pl.BlockSpec((None, SG_GROUPS, BLOCK, 1), lambda i: (layer, 0, 0, 0))],
        out_specs=pl.BlockSpec((tokens, SG_W), lambda i: (i, 0)),
        out_shape=jax.ShapeDtypeStruct((n, SG_W), bf16),
        compiler_params=_params(("parallel",), 32),
        name="mix_sgu",
    )(z2, z2, ln_g, ln_b, sg_w, sg_b)


def _padded(t):
    return t + t // ATT_PAD_EVERY


def _attention_kernel(q_ref, k_ref, v_ref, bias_ref, o_ref, kp_ref, vp_ref, kd_ref, vd_ref, qp_ref, stage_ref,
                      mp_ref, lp_ref, accp_ref, mn_ref, ln_ref, accn_ref):
    sb = pl.program_id(2)
    scale = ATT_HEAD_DIM ** -0.5
    quarter = ATT_SUPER // ATT_PAD_EVERY
    super_p = _padded(ATT_SUPER)
    cur_base = lax.rem(sb, 2) * super_p
    prev_base = super_p - cur_base

    def pad_rows(dst_ref, dst_base):
        for j in range(ATT_PAD_EVERY):
            dst_ref[pl.ds(dst_base + j, quarter, stride=ATT_PAD_EVERY + 1), :] = (
                stage_ref[pl.ds(j, quarter, stride=ATT_PAD_EVERY), :])

    @pl.when(sb == 0)
    def _():
        kd_ref[pl.ds(0, BLOCK), :] = k_ref[pl.ds(0, BLOCK), :]
        vd_ref[pl.ds(0, BLOCK), :] = v_ref[pl.ds(0, BLOCK), :]

    @pl.when(sb > 0)
    def _():
        kd_ref[pl.ds(0, BLOCK), :] = kd_ref[pl.ds(ATT_SUPER, BLOCK), :]
        vd_ref[pl.ds(0, BLOCK), :] = vd_ref[pl.ds(ATT_SUPER, BLOCK), :]

    kd_ref[pl.ds(BLOCK, ATT_SUPER), :] = k_ref[...]
    vd_ref[pl.ds(BLOCK, ATT_SUPER), :] = v_ref[...]
    for src_ref, dst_ref, base in ((k_ref, kp_ref, cur_base), (v_ref, vp_ref, cur_base), (q_ref, qp_ref, 0)):
        stage_ref[...] = src_ref[...].astype(f32)
        pad_rows(dst_ref, base)

    def softmax(s):
        mb = jnp.max(s, axis=-1, keepdims=True)
        p = jnp.exp(s - mb)
        return mb, jnp.sum(p, axis=-1, keepdims=True), p.astype(bf16)

    def merged(state, mb, lb, ob):
        mb = jnp.broadcast_to(mb, (BLOCK, ATT_HEAD_DIM))
        lb = jnp.broadcast_to(lb, (BLOCK, ATT_HEAD_DIM))
        if state is None:
            return mb, lb, ob
        m_old, l_old, acc_old = state
        m_new = jnp.maximum(m_old, mb)
        wa = jnp.exp(m_old - m_new)
        wb = jnp.exp(mb - m_new)
        return m_new, wa * l_old + wb * lb, wa * acc_old + wb * ob

    def run_pass(nblocks, group, load, finish):
        def body(step, carry):
            its = [step * group + g for g in range(group)]
            ins = [load(it) for it in its]
            logit = [lax.dot_general(q, k, (((1,), (1,)), ((), ())), preferred_element_type=f32) * scale + b
                     for q, k, _, b, _ in ins]
            soft = [softmax(s) for s in logit]
            outs = [jnp.dot(soft[g][2], ins[g][2], preferred_element_type=f32) for g in range(group)]
            new = [merged(ins[g][4], soft[g][0], soft[g][1], outs[g]) for g in range(group)]
            for g in range(group):
                finish(its[g], *new[g])
            return carry

        lax.fori_loop(0, nblocks // group, body, 0)

    for bi, (window, d) in reversed(list(enumerate(DILATION_CONFIGS))[1:]):
        assert window // d == BLOCK and ATT_SUPER % (d * BLOCK) == 0 and d % ATT_PAD_EVERY == 0
        span_p = _padded(d * BLOCK)
        blocks_per_super = ATT_SUPER // (d * BLOCK)
        shift = d.bit_length() - 1
        stride = _padded(d)
        is_first_pass = bi == len(DILATION_CONFIGS) - 1

        def rows(start, stride=stride):
            return pl.ds(start, BLOCK, stride=stride)

        def load(it, bi=bi, d=d, span_p=span_p, blocks_per_super=blocks_per_super, shift=shift,
                 rows=rows, is_first_pass=is_first_pass):
            n_loc = lax.shift_right_logical(it, shift)
            r = lax.bitwise_and(it, d - 1)
            r_p = r + lax.shift_right_logical(r, 2)
            first = jnp.logical_and(sb == 0, n_loc == 0)
            q_rows = rows(n_loc * span_p + r_p)
            cur = cur_base + n_loc * span_p + r_p
            before = jnp.where(sb == 0, cur_base, prev_base + (blocks_per_super - 1) * span_p)
            prev = jnp.where(n_loc > 0, cur_base + (n_loc - 1) * span_p, before) + r_p
            k = jnp.concatenate([kp_ref[rows(prev), :], kp_ref[rows(cur), :]], axis=0).astype(bf16)
            v = jnp.concatenate([vp_ref[rows(prev), :], vp_ref[rows(cur), :]], axis=0).astype(bf16)
            state = None if is_first_pass else (mp_ref[q_rows, :], lp_ref[q_rows, :], accp_ref[q_rows, :])
            return qp_ref[q_rows, :].astype(bf16), k, v, bias_ref[bi, first.astype(jnp.int32)], state

        def finish(it, m, l, acc, bi=bi, d=d, span_p=span_p, shift=shift, rows=rows):
            n_loc = lax.shift_right_logical(it, shift)
            r = lax.bitwise_and(it, d - 1)
            if bi == 1:
                q_rows = pl.ds(n_loc * (d * BLOCK) + r, BLOCK, stride=d)
                mn_ref[q_rows, :], ln_ref[q_rows, :], accn_ref[q_rows, :] = m, l, acc
            else:
                q_rows = rows(n_loc * span_p + r + lax.shift_right_logical(r, 2))
                mp_ref[q_rows, :], lp_ref[q_rows, :], accp_ref[q_rows, :] = m, l, acc

        run_pass(blocks_per_super * d, ATT_GROUP_FIRST if is_first_pass else ATT_GROUP, load, finish)

    window, d = DILATION_CONFIGS[0]
    assert d == 1 and window == BLOCK and len(DILATION_CONFIGS) == 3

    def load_dense(it):
        q_rows = pl.ds(pl.multiple_of(it * BLOCK, BLOCK), BLOCK)
        kv_rows = pl.ds(pl.multiple_of(it * BLOCK, BLOCK), 2 * BLOCK)
        first = jnp.logical_and(sb == 0, it == 0).astype(jnp.int32)
        return (q_ref[q_rows, :], kd_ref[kv_rows, :], vd_ref[kv_rows, :], bias_ref[0, first],
                (mn_ref[q_rows, :], ln_ref[q_rows, :], accn_ref[q_rows, :]))

    def finish_dense(it, m, l, acc):
        o_ref[pl.ds(pl.multiple_of(it * BLOCK, BLOCK), BLOCK), :] = (acc / l).astype(o_ref.dtype)

    run_pass(ATT_SUPER // BLOCK, ATT_GROUP, load_dense, finish_dense)


def _t5_bucket(dist):
    max_exact = REL_BUCKETS // 2
    d_f = jnp.maximum(dist, 1).astype(f32)
    large = max_exact + (jnp.log(d_f / max_exact) / math.log(REL_MAX_DIST / max_exact)
                         * (REL_BUCKETS - max_exact)).astype(jnp.int32)
    large = jnp.minimum(large, REL_BUCKETS - 1)
    return jnp.where(dist < max_exact, dist, large)


def _attention_bias(rel_bias):
    i = jnp.arange(BLOCK)[:, None]
    j = jnp.arange(2 * BLOCK)[None, :]
    steps = BLOCK + i - j
    band = (steps >= 0) & (steps <= BLOCK)
    tab = rel_bias.astype(f32)
    out = []
    for _, d in DILATION_CONFIGS:
        onehot = jax.nn.one_hot(_t5_bucket(d * jnp.maximum(steps, 0)), REL_BUCKETS, dtype=f32)
        bias = jnp.einsum('qkb,bh->hqk', onehot, tab, precision=lax.Precision.HIGHEST)
        later = jnp.where(band[None], bias, NEG_INF)
        first = jnp.where((band & (j >= BLOCK))[None], bias, NEG_INF)
        out.append(jnp.stack([later, first], axis=1))
    return jnp.stack(out)


def dilated_attention(z, bias):
    b, s, _ = z.shape
    off = (2 * RET_QK_W + 2 * RET_V_W + 2 * SG_W) // ATT_HEAD_DIM

    def tile(col):
        return pl.BlockSpec((None, ATT_SUPER, ATT_HEAD_DIM), lambda bi, hd, t: (bi, t, col + hd))

    return pl.pallas_call(
        _attention_kernel,
        grid=(b, ATT_HEADS, s // ATT_SUPER),
        in_specs=[tile(off), tile(off + ATT_HEADS), tile(off + 2 * ATT_HEADS),
                  pl.BlockSpec((N_BRANCH, None, 2, BLOCK, 2 * BLOCK), lambda bi, hd, t: (0, hd, 0, 0, 0))],
        out_specs=tile(0),
        out_shape=jax.ShapeDtypeStruct((b, s, ATT_W), bf16),
        scratch_shapes=([pltpu.VMEM((2 * _padded(ATT_SUPER), ATT_HEAD_DIM), f32)] * 2
                        + [pltpu.VMEM((BLOCK + ATT_SUPER, ATT_HEAD_DIM), bf16)] * 2
                        + [pltpu.VMEM((_padded(ATT_SUPER), ATT_HEAD_DIM), f32)]
                        + [pltpu.VMEM((ATT_SUPER, ATT_HEAD_DIM), f32)]
                        + [pltpu.VMEM((_padded(ATT_SUPER), ATT_HEAD_DIM), f32)] * 3
                        + [pltpu.VMEM((ATT_SUPER, ATT_HEAD_DIM), f32)] * 3),
        compiler_params=_params(("parallel", "parallel", "arbitrary"), 32),
        name="mix_attention",
    )(z, z, z, bias)


def _merge_kernel(ret_ref, sg_ref, att_ref, gate_ref, wr_ref, ws_ref, wa_ref, wo_ref, x_ref, g_ref,
                  xo_ref, ho_ref):
    d = D_MODEL
    merged = gate_ref[:, 0:d].astype(f32) * jnp.dot(ret_ref[...], wr_ref[...], preferred_element_type=f32)
    merged += gate_ref[:, d:2 * d].astype(f32) * jnp.dot(sg_ref[...], ws_ref[...], preferred_element_type=f32)
    merged += gate_ref[:, 2 * d:3 * d].astype(f32) * jnp.dot(att_ref[...], wa_ref[...], preferred_element_type=f32)
    xn = x_ref[...] + jnp.dot(merged.astype(bf16), wo_ref[...], preferred_element_type=f32)
    xo_ref[...] = xn
    ho_ref[...] = _rms(xn, g_ref[...]).astype(ho_ref.dtype)


def merge_project(ret, sg, att, gates, w_ret, w_sg, w_att, w_out, layer, x, g_next, tm=256):
    n, d = x.shape

    def resident(w):
        return pl.BlockSpec((None,) + w.shape[1:], lambda i: (layer, 0, 0), pipeline_mode=pl.Buffered(1))

    def rows(w):
        return pl.BlockSpec((tm, w), lambda i: (i, 0))

    return pl.pallas_call(
        _merge_kernel,
        grid=(n // tm,),
        in_specs=[rows(RET_V_W), rows(SG_W), rows(ATT_W), rows(GATE_W),
                  resident(w_ret), resident(w_sg), resident(w_att), resident(w_out),
                  rows(d), pl.BlockSpec((1, d), lambda i: (0, 0))],
        out_specs=[rows(d), rows(d)],
        out_shape=[jax.ShapeDtypeStruct((n, d), f32), jax.ShapeDtypeStruct((n, d), bf16)],
        compiler_params=_params(("parallel",), 60),
        name="mix_merge",
    )(ret, sg, att, gates, w_ret, w_sg, w_att, w_out, x, g_next)


def kernel(x, ffn1_norm, ffn1_w_gate, ffn1_w_up, ffn1_w_down, mix_norm, w_in, b_gate, sg_ln_g, sg_ln_b,
           sg_w, sg_b, rel_bias, w_proj_ret, w_proj_sg, w_proj_att, w_out, ffn2_norm, ffn2_w_gate,
           ffn2_w_up, ffn2_w_down, final_norm):
    b, s, d = x.shape
    n = b * s
    depth = ffn1_norm.shape[0]

    def w16(w):
        return w.astype(bf16)

    ffn1_wd, ffn2_wd = w16(ffn1_w_down), w16(ffn2_w_down)
    w_ret16, w_sg16, w_att16, w_out16 = w16(w_proj_ret), w16(w_proj_sg), w16(w_proj_att), w16(w_out)
    b_gate3 = b_gate[:, None, :]
    ln_g3, ln_b3 = sg_ln_g[:, None, :], sg_ln_b[:, None, :]
    sg_b4 = sg_b[..., None]
    tables = _retention_tables()
    cos, sin = _rotary_tables(s)
    att_bias = _attention_bias(rel_bias)

    xr = x.reshape(n, d)
    h = rmsnorm(xr, ffn1_norm[0][None], bf16)
    for l in range(depth):
        a = gateup(h, ffn1_w_gate, ffn1_w_up, l)
        xr, h = down_residual_norm(a, ffn1_wd, l, xr, mix_norm[l][None], bf16)

        z = in_projection(h, w_in, l, cos, sin)
        z3 = z.reshape(b, s, Z_W)
        ret = retention(z3, tables).reshape(n, RET_V_W)
        sg = spatial_gating(z, ln_g3, ln_b3, sg_w, sg_b4, l)
        att = dilated_attention(z3, att_bias).reshape(n, ATT_W)
        gates = in_gates(h, w_in, b_gate3, l)
        xr, h = merge_project(ret, sg, att, gates, w_ret16, w_sg16, w_att16, w_out16, l, xr,
                              ffn2_norm[l][None])

        a = gateup(h, ffn2_w_gate, ffn2_w_up, l)
        last = l == depth - 1
        g_next = final_norm[None] if last else ffn1_norm[l + 1][None]
        xr, h = down_residual_norm(a, ffn2_wd, l, xr, g_next, f32 if last else bf16)
    return h.reshape(b, s, d)
```

```python
import functools
import math

import jax
import jax.numpy as jnp
from jax import lax
from jax.experimental import pallas as pl
from jax.experimental.pallas import tpu as pltpu

f32 = jnp.float32
bf16 = jnp.bfloat16

D_MODEL = 2048
DEPTH = 4
BLOCK = 128
RET_HEADS = 4
RET_QK_DIM = 256
RET_V_DIM = 512
SG_GROUPS = 4
SG_GROUP_DIM = 256
ATT_HEADS = 8
ATT_HEAD_DIM = 128
DILATION_CONFIGS = ((128, 1), (512, 4), (2048, 16))
REL_BUCKETS = 32
REL_MAX_DIST = 2048
D_FF = 5632
ROPE_BASE = 10000.0
EPS = 1e-6
NEG_INF = -1e30

RET_QK_W = RET_HEADS * RET_QK_DIM
RET_V_W = RET_HEADS * RET_V_DIM
SG_W = SG_GROUPS * SG_GROUP_DIM
ATT_W = ATT_HEADS * ATT_HEAD_DIM
N_BRANCH = 3
Z_W = 2 * RET_QK_W + 2 * RET_V_W + 2 * SG_W + 3 * ATT_W
GATE_W = N_BRANCH * D_MODEL

ATT_SUPER = 2048
ATT_GROUP = 4
ATT_GROUP_FIRST = 8
ATT_PAD_EVERY = 4
MM_ROW_CHUNK = 256
RESIDENT_STAGE_ROWS = 256
V7X_VMEM_LIMIT = 60 * 1024 * 1024


def _params(sem, vmem_mb=None):
    return pltpu.CompilerParams(
        dimension_semantics=sem,
        vmem_limit_bytes=None if vmem_mb is None else min(vmem_mb * 1024 * 1024, V7X_VMEM_LIMIT))


def _rms(x, g):
    return x * lax.rsqrt(jnp.mean(x * x, axis=-1, keepdims=True) + EPS) * g


def _rmsnorm_kernel(x_ref, g_ref, o_ref):
    o_ref[...] = _rms(x_ref[...], g_ref[...]).astype(o_ref.dtype)


def rmsnorm(x, g, out_dtype, tm=512):
    n, d = x.shape
    return pl.pallas_call(
        _rmsnorm_kernel,
        grid=(n // tm,),
        in_specs=[pl.BlockSpec((tm, d), lambda i: (i, 0)),
                  pl.BlockSpec((1, d), lambda i: (0, 0))],
        out_specs=pl.BlockSpec((tm, d), lambda i: (i, 0)),
        out_shape=jax.ShapeDtypeStruct((n, d), out_dtype),
        compiler_params=_params(("parallel",)),
        name="rmsnorm",
    )(x, g)


def _row_chunks(rows):
    return [pl.ds(c * MM_ROW_CHUNK, MM_ROW_CHUNK) for c in range(rows // MM_ROW_CHUNK)]


def _gateup_kernel(h_ref, wg_ref, wu_ref, o_ref, wg16_ref, wu16_ref):
    @pl.when(pl.program_id(1) == 0)
    def _():
        wg16_ref[...] = wg_ref[...].astype(bf16)
        wu16_ref[...] = wu_ref[...].astype(bf16)

    for rows in _row_chunks(h_ref.shape[0]):
        h = h_ref[rows, :]
        g = jnp.dot(h, wg16_ref[...], preferred_element_type=f32)
        u = jnp.dot(h, wu16_ref[...], preferred_element_type=f32)
        o_ref[rows, :] = (g * jax.nn.sigmoid(g) * u).astype(o_ref.dtype)


def gateup(h, wg, wu, layer, tm=2048, tf=512):
    n, d = h.shape
    dff = wg.shape[-1]
    return pl.pallas_call(
        _gateup_kernel,
        grid=(dff // tf, n // tm),
        in_specs=[pl.BlockSpec((tm, d), lambda j, i: (i, 0)),
                  pl.BlockSpec((None, d, tf), lambda j, i: (layer, 0, j)),
                  pl.BlockSpec((None, d, tf), lambda j, i: (layer, 0, j))],
        out_specs=pl.BlockSpec((tm, tf), lambda j, i: (i, j)),
        out_shape=jax.ShapeDtypeStruct((n, dff), bf16),
        scratch_shapes=[pltpu.VMEM((d, tf), bf16), pltpu.VMEM((d, tf), bf16)],
        compiler_params=_params(("arbitrary", "arbitrary"), 56),
        name="ffn_gateup",
    )(h, wg, wu)


def _round_weight_into(w_hbm, layer, w16_ref, stage_ref, sem):
    rows = stage_ref.shape[1]
    nchunks = w16_ref.shape[0] // rows

    def copy(c):
        return pltpu.make_async_copy(w_hbm.at[layer, pl.ds(c * rows, rows), :], stage_ref.at[c % 2], sem.at[c % 2])

    copy(0).start()
    for c in range(nchunks):
        if c + 1 < nchunks:
            copy(c + 1).start()
        copy(c).wait()
        w16_ref[pl.ds(c * rows, rows), :] = stage_ref[c % 2].astype(bf16)


def _down_kernel(a_ref, w_hbm, x_ref, g_ref, xo_ref, ho_ref, w16_ref, stage_ref, sem, *, layer):
    @pl.when(pl.program_id(0) == 0)
    def _():
        _round_weight_into(w_hbm, layer, w16_ref, stage_ref, sem)

    acc = jnp.dot(a_ref[...], w16_ref[...], preferred_element_type=f32)
    xn = x_ref[...] + 0.5 * acc
    xo_ref[...] = xn
    ho_ref[...] = _rms(xn, g_ref[...]).astype(ho_ref.dtype)


def down_residual_norm(a, wd, layer, x, g_next, h_dtype, tm=256):
    n, dff = a.shape
    d = wd.shape[-1]
    return pl.pallas_call(
        functools.partial(_down_kernel, layer=layer),
        grid=(n // tm,),
        in_specs=[pl.BlockSpec((tm, dff), lambda i: (i, 0)),
                  pl.BlockSpec(memory_space=pl.ANY),
                  pl.BlockSpec((tm, d), lambda i: (i, 0)),
                  pl.BlockSpec((1, d), lambda i: (0, 0))],
        out_specs=[pl.BlockSpec((tm, d), lambda i: (i, 0)),
                   pl.BlockSpec((tm, d), lambda i: (i, 0))],
        out_shape=[jax.ShapeDtypeStruct((n, d), f32),
                   jax.ShapeDtypeStruct((n, d), h_dtype)],
        scratch_shapes=[pltpu.VMEM((dff, d), bf16),
                        pltpu.VMEM((2, RESIDENT_STAGE_ROWS, d), f32),
                        pltpu.SemaphoreType.DMA((2,))],
        compiler_params=_params(("arbitrary",), 56),
        name="ffn_down",
    )(a, wd, x, g_next)


def _stream_project(h_ref, w16_ref, o_ref, epilogue):
    for rows in _row_chunks(h_ref.shape[0]):
        acc = jnp.dot(h_ref[rows, :], w16_ref[...], preferred_element_type=f32)
        o_ref[rows, :] = epilogue(rows, acc).astype(o_ref.dtype)


def _inproj_kernel(h_ref, w_ref, cos_ref, sin_ref, o_ref, w16_ref, *, tn):
    j = pl.program_id(0)

    @pl.when(pl.program_id(1) == 0)
    def _():
        w16_ref[...] = w_ref[...].astype(bf16)

    q_tiles = RET_QK_W // tn
    rot_tiles = 2 * q_tiles
    gate_lo = (2 * RET_QK_W + RET_V_W) // tn
    gate_hi = gate_lo + RET_V_W // tn
    half = RET_QK_DIM // 2

    @pl.when(j < rot_tiles)
    def _():
        kscale = jnp.where(j >= q_tiles, RET_QK_DIM ** -0.5, 1.0).astype(f32)

        def rotate(rows, acc):
            cos = cos_ref[rows, :] * kscale
            sin = sin_ref[rows, :] * kscale
            out = []
            for hd in range(tn // RET_QK_DIM):
                t1 = acc[:, hd * RET_QK_DIM:hd * RET_QK_DIM + half]
                t2 = acc[:, hd * RET_QK_DIM + half:(hd + 1) * RET_QK_DIM]
                out += [t1 * cos - t2 * sin, t1 * sin + t2 * cos]
            return jnp.concatenate(out, axis=-1)

        _stream_project(h_ref, w16_ref, o_ref, rotate)

    @pl.when((j >= gate_lo) & (j < gate_hi))
    def _():
        _stream_project(h_ref, w16_ref, o_ref, lambda rows, acc: acc * jax.nn.sigmoid(acc))

    @pl.when(((j >= rot_tiles) & (j < gate_lo)) | (j >= gate_hi))
    def _():
        _stream_project(h_ref, w16_ref, o_ref, lambda rows, acc: acc)


def _ingate_kernel(h_ref, w_ref, b_ref, o_ref, w16_ref):
    @pl.when(pl.program_id(1) == 0)
    def _():
        w16_ref[...] = w_ref[...].astype(bf16)

    _stream_project(h_ref, w16_ref, o_ref, lambda rows, acc: jax.nn.sigmoid(acc + b_ref[...]))


def in_projection(h, w_in, layer, cos, sin, tm=2048, tn=1024):
    n, d = h.shape
    seq_tiles = cos.shape[0] // tm
    table = pl.BlockSpec((tm, RET_QK_DIM // 2), lambda j, i: (i % seq_tiles, 0))
    return pl.pallas_call(
        functools.partial(_inproj_kernel, tn=tn),
        grid=(Z_W // tn, n // tm),
        in_specs=[pl.BlockSpec((tm, d), lambda j, i: (i, 0)),
                  pl.BlockSpec((None, d, tn), lambda j, i: (layer, 0, j)),
                  table, table],
        out_specs=pl.BlockSpec((tm, tn), lambda j, i: (i, j)),
        out_shape=jax.ShapeDtypeStruct((n, Z_W), bf16),
        scratch_shapes=[pltpu.VMEM((d, tn), bf16)],
        compiler_params=_params(("arbitrary", "arbitrary"), 58),
        name="mix_inproj",
    )(h, w_in, cos, sin)


def in_gates(h, w_in, b_gate, layer, tm=2048, tn=1024):
    n, d = h.shape
    off = Z_W // tn
    return pl.pallas_call(
        _ingate_kernel,
        grid=(GATE_W // tn, n // tm),
        in_specs=[pl.BlockSpec((tm, d), lambda j, i: (i, 0)),
                  pl.BlockSpec((None, d, tn), lambda j, i: (layer, 0, j + off)),
                  pl.BlockSpec((None, 1, tn), lambda j, i: (layer, 0, j))],
        out_specs=pl.BlockSpec((tm, tn), lambda j, i: (i, j)),
        out_shape=jax.ShapeDtypeStruct((n, GATE_W), bf16),
        scratch_shapes=[pltpu.VMEM((d, tn), bf16)],
        compiler_params=_params(("arbitrary", "arbitrary"), 58),
        name="mix_ingate",
    )(h, w_in, b_gate)


def _retention_kernel(q_ref, k_ref, v_ref, g_ref, inner_ref, xi_ref, zeta_ref, gch_ref, o_ref, state_ref,
                      *, nchunk):
    @pl.when(pl.program_id(2) == 0)
    def _():
        state_ref[...] = jnp.zeros_like(state_ref)

    inner = inner_ref[...]
    xi = xi_ref[...]
    zeta = zeta_ref[...]
    gch = gch_ref[...]
    chunks = range(nchunk)
    rows = [pl.ds(c * BLOCK, BLOCK) for c in chunks]

    q = [q_ref[rows[c], :] for c in chunks]
    k = [k_ref[rows[c], :] for c in chunks]
    v = [v_ref[rows[c], :] for c in chunks]
    a = [lax.dot_general(q[c], k[c], (((1,), (1,)), ((), ())), preferred_element_type=f32) * inner
         for c in chunks]
    upd = [jnp.dot((k[c].astype(f32) * zeta).T.astype(bf16), v[c], preferred_element_type=f32)
           for c in chunks]
    local = [jnp.dot(a[c].astype(bf16), v[c], preferred_element_type=f32) for c in chunks]
    state = state_ref[...]
    carried = []
    for c in chunks:
        carried.append(jnp.dot(q[c], state.astype(bf16), preferred_element_type=f32))
        state = state * gch + upd[c]
    state_ref[...] = state
    for c in chunks:
        o = local[c] + carried[c] * xi
        on = o * lax.rsqrt(jnp.mean(o * o, axis=-1, keepdims=True) + EPS)
        o_ref[rows[c], :] = (g_ref[rows[c], :].astype(f32) * on).astype(o_ref.dtype)


def _retention_tables():
    h = jnp.arange(RET_HEADS, dtype=f32)
    log_g = jnp.log1p(-(2.0 ** (-5.0 - h)))
    idx = jnp.arange(BLOCK, dtype=f32)
    rel = idx[:, None] - idx[None, :]
    inner = jnp.where(rel[None] >= 0, jnp.exp(jnp.maximum(rel, 0.0)[None] * log_g[:, None, None]), 0.0)
    xi = jnp.exp((idx + 1.0)[None, :] * log_g[:, None])[..., None]
    zeta = jnp.exp((BLOCK - 1.0 - idx)[None, :] * log_g[:, None])[..., None]
    gch = jnp.broadcast_to(jnp.exp(BLOCK * log_g)[:, None, None], (RET_HEADS, 1, RET_V_DIM))
    return inner, xi, zeta, gch


def _rotary_tables(seq):
    inv = ROPE_BASE ** (-jnp.arange(0, RET_QK_DIM, 2, dtype=f32) / RET_QK_DIM)
    ang = jnp.arange(seq).astype(f32)[:, None] * inv[None, :]
    return jnp.cos(ang), jnp.sin(ang)


def retention(z, tables, tokens=1024):
    b, s, _ = z.shape
    inner, xi, zeta, gch = tables
    nv = 2 * RET_QK_W // RET_V_DIM
    kern = functools.partial(_retention_kernel, nchunk=tokens // BLOCK)
    return pl.pallas_call(
        kern,
        grid=(b, RET_HEADS, s // tokens),
        in_specs=[pl.BlockSpec((None, tokens, RET_QK_DIM), lambda bi, h, t: (bi, t, h)),
                  pl.BlockSpec((None, tokens, RET_QK_DIM), lambda bi, h, t: (bi, t, RET_HEADS + h)),
                  pl.BlockSpec((None, tokens, RET_V_DIM), lambda bi, h, t: (bi, t, nv + h)),
                  pl.BlockSpec((None, tokens, RET_V_DIM), lambda bi, h, t: (bi, t, nv + RET_HEADS + h)),
                  pl.BlockSpec((None, BLOCK, BLOCK), lambda bi, h, t: (h, 0, 0)),
                  pl.BlockSpec((None, BLOCK, 1), lambda bi, h, t: (h, 0, 0)),
                  pl.BlockSpec((None, BLOCK, 1), lambda bi, h, t: (h, 0, 0)),
                  pl.BlockSpec((None, 1, RET_V_DIM), lambda bi, h, t: (h, 0, 0))],
        out_specs=pl.BlockSpec((None, tokens, RET_V_DIM), lambda bi, h, t: (bi, t, h)),
        out_shape=jax.ShapeDtypeStruct((b, s, RET_V_W), bf16),
        scratch_shapes=[pltpu.VMEM((RET_QK_DIM, RET_V_DIM), f32)],
        compiler_params=_params(("parallel", "parallel", "arbitrary"), 32),
        name="mix_retention",
    )(z, z, z, z, inner, xi, zeta, gch)


def _sgu_kernel(u_ref, v_ref, lg_ref, lb_ref, w_ref, b_ref, o_ref, *, nchunk):
    u = jax.nn.gelu(u_ref[...].astype(f32))
    v = jax.nn.gelu(v_ref[...].astype(f32))
    mu = jnp.mean(v, axis=-1, keepdims=True)
    var = jnp.mean(jnp.square(v - mu), axis=-1, keepdims=True)
    vn = ((v - mu) * lax.rsqrt(var + EPS) * lg_ref[...] + lb_ref[...]).astype(bf16)
    row = lax.broadcasted_iota(jnp.int32, (BLOCK, BLOCK), 0)
    col = lax.broadcasted_iota(jnp.int32, (BLOCK, BLOCK), 1)
    causal = row >= col
    for g in range(SG_GROUPS):
        wc = jnp.where(causal, w_ref[g], 0.0).astype(bf16)
        bias = b_ref[g]
        cols = slice(g * SG_GROUP_DIM, (g + 1) * SG_GROUP_DIM)
        for c in range(nchunk):
            rows = slice(c * BLOCK, (c + 1) * BLOCK)
            mix = jnp.dot(wc, vn[rows, cols], preferred_element_type=f32) + bias
            o_ref[rows, cols] = (u[rows, cols] * mix).astype(o_ref.dtype)


def spatial_gating(z2, ln_g, ln_b, sg_w, sg_b, layer, tokens=512):
    n = z2.shape[0]
    off = (2 * RET_QK_W + 2 * RET_V_W) // SG_W
    kern = functools.partial(_sgu_kernel, nchunk=tokens // BLOCK)
    return pl.pallas_call(
        kern,
        grid=(n // tokens,),
        in_specs=[pl.BlockSpec((tokens, SG_W), lambda i: (i, off)),
                  pl.BlockSpec((tokens, SG_W), lambda i: (i, off + 1)),
                  pl.BlockSpec((None, 1, SG_W), lambda i: (layer, 0, 0)),
                  pl.BlockSpec((None, 1, SG_W), lambda i: (layer, 0, 0)),
                  pl.BlockSpec((None, SG_GROUPS, BLOCK, BLOCK), lambda i: (layer, 0, 0, 0)),
                  pl.BlockSpec((None, SG_GROUPS, BLOCK, 1), lambda i: (layer, 0, 0, 0))],
        out_specs=pl.BlockSpec((tokens, SG_W), lambda i: (i, 0)),
        out_shape=jax.ShapeDtypeStruct((n, SG_W), bf16),
        compiler_params=_params(("parallel",), 32),
        name="mix_sgu",
    )(z2, z2, ln_g, ln_b, sg_w, sg_b)


def _padded(t):
    return t + t // ATT_PAD_EVERY


def _attention_kernel(q_ref, k_ref, v_ref, bias_ref, o_ref, kp_ref, vp_ref, kd_ref, vd_ref, qp_ref, stage_ref,
                      mp_ref, lp_ref, accp_ref, mn_ref, ln_ref, accn_ref):
    sb = pl.program_id(2)
    scale = ATT_HEAD_DIM ** -0.5
    quarter = ATT_SUPER // ATT_PAD_EVERY
    super_p = _padded(ATT_SUPER)
    cur_base = lax.rem(sb, 2) * super_p
    prev_base = super_p - cur_base

    def pad_rows(dst_ref, dst_base):
        for j in range(ATT_PAD_EVERY):
            dst_ref[pl.ds(dst_base + j, quarter, stride=ATT_PAD_EVERY + 1), :] = (
                stage_ref[pl.ds(j, quarter, stride=ATT_PAD_EVERY), :])

    @pl.when(sb == 0)
    def _():
        kd_ref[pl.ds(0, BLOCK), :] = k_ref[pl.ds(0, BLOCK), :]
        vd_ref[pl.ds(0, BLOCK), :] = v_ref[pl.ds(0, BLOCK), :]

    @pl.when(sb > 0)
    def _():
        kd_ref[pl.ds(0, BLOCK), :] = kd_ref[pl.ds(ATT_SUPER, BLOCK), :]
        vd_ref[pl.ds(0, BLOCK), :] = vd_ref[pl.ds(ATT_SUPER, BLOCK), :]

    kd_ref[pl.ds(BLOCK, ATT_SUPER), :] = k_ref[...]
    vd_ref[pl.ds(BLOCK, ATT_SUPER), :] = v_ref[...]
    for src_ref, dst_ref, base in ((k_ref, kp_ref, cur_base), (v_ref, vp_ref, cur_base), (q_ref, qp_ref, 0)):
        stage_ref[...] = src_ref[...].astype(f32)
        pad_rows(dst_ref, base)

    def softmax(s):
        mb = jnp.max(s, axis=-1, keepdims=True)
        p = jnp.exp(s - mb)
        return mb, jnp.sum(p, axis=-1, keepdims=True), p.astype(bf16)

    def merged(state, mb, lb, ob):
        mb = jnp.broadcast_to(mb, (BLOCK, ATT_HEAD_DIM))
        lb = jnp.broadcast_to(lb, (BLOCK, ATT_HEAD_DIM))
        if state is None:
            return mb, lb, ob
        m_old, l_old, acc_old = state
        m_new = jnp.maximum(m_old, mb)
        wa = jnp.exp(m_old - m_new)
        wb = jnp.exp(mb - m_new)
        return m_new, wa * l_old + wb * lb, wa * acc_old + wb * ob

    def run_pass(nblocks, group, load, finish):
        def body(step, carry):
            its = [step * group + g for g in range(group)]
            ins = [load(it) for it in its]
            logit = [lax.dot_general(q, k, (((1,), (1,)), ((), ())), preferred_element_type=f32) * scale + b
                     for q, k, _, b, _ in ins]
            soft = [softmax(s) for s in logit]
            outs = [jnp.dot(soft[g][2], ins[g][2], preferred_element_type=f32) for g in range(group)]
            new = [merged(ins[g][4], soft[g][0], soft[g][1], outs[g]) for g in range(group)]
            for g in range(group):
                finish(its[g], *new[g])
            return carry

        lax.fori_loop(0, nblocks // group, body, 0)

    for bi, (window, d) in reversed(list(enumerate(DILATION_CONFIGS))[1:]):
        assert window // d == BLOCK and ATT_SUPER % (d * BLOCK) == 0 and d % ATT_PAD_EVERY == 0
        span_p = _padded(d * BLOCK)
        blocks_per_super = ATT_SUPER // (d * BLOCK)
        shift = d.bit_length() - 1
        stride = _padded(d)
        is_first_pass = bi == len(DILATION_CONFIGS) - 1

        def rows(start, stride=stride):
            return pl.ds(start, BLOCK, stride=stride)

        def load(it, bi=bi, d=d, span_p=span_p, blocks_per_super=blocks_per_super, shift=shift,
                 rows=rows, is_first_pass=is_first_pass):
            n_loc = lax.shift_right_logical(it, shift)
            r = lax.bitwise_and(it, d - 1)
            r_p = r + lax.shift_right_logical(r, 2)
            first = jnp.logical_and(sb == 0, n_loc == 0)
            q_rows = rows(n_loc * span_p + r_p)
            cur = cur_base + n_loc * span_p + r_p
            before = jnp.where(sb == 0, cur_base, prev_base + (blocks_per_super - 1) * span_p)
            prev = jnp.where(n_loc > 0, cur_base + (n_loc - 1) * span_p, before) + r_p
            k = jnp.concatenate([kp_ref[rows(prev), :], kp_ref[rows(cur), :]], axis=0).astype(bf16)
            v = jnp.concatenate([vp_ref[rows(prev), :], vp_ref[rows(cur), :]], axis=0).astype(bf16)
            state = None if is_first_pass else (mp_ref[q_rows, :], lp_ref[q_rows, :], accp_ref[q_rows, :])
            return qp_ref[q_rows, :].astype(bf16), k, v, bias_ref[bi, first.astype(jnp.int32)], state

        def finish(it, m, l, acc, bi=bi, d=d, span_p=span_p, shift=shift, rows=rows):
            n_loc = lax.shift_right_logical(it, shift)
            r = lax.bitwise_and(it, d - 1)
            if bi == 1:
                q_rows = pl.ds(n_loc * (d * BLOCK) + r, BLOCK, stride=d)
                mn_ref[q_rows, :], ln_ref[q_rows, :], accn_ref[q_rows, :] = m, l, acc
            else:
                q_rows = rows(n_loc * span_p + r + lax.shift_right_logical(r, 2))
                mp_ref[q_rows, :], lp_ref[q_rows, :], accp_ref[q_rows, :] = m, l, acc

        run_pass(blocks_per_super * d, ATT_GROUP_FIRST if is_first_pass else ATT_GROUP, load, finish)

    window, d = DILATION_CONFIGS[0]
    assert d == 1 and window == BLOCK and len(DILATION_CONFIGS) == 3

    def load_dense(it):
        q_rows = pl.ds(pl.multiple_of(it * BLOCK, BLOCK), BLOCK)
        kv_rows = pl.ds(pl.multiple_of(it * BLOCK, BLOCK), 2 * BLOCK)
        first = jnp.logical_and(sb == 0, it == 0).astype(jnp.int32)
        return (q_ref[q_rows, :], kd_ref[kv_rows, :], vd_ref[kv_rows, :], bias_ref[0, first],
                (mn_ref[q_rows, :], ln_ref[q_rows, :], accn_ref[q_rows, :]))

    def finish_dense(it, m, l, acc):
        o_ref[pl.ds(pl.multiple_of(it * BLOCK, BLOCK), BLOCK), :] = (acc / l).astype(o_ref.dtype)

    run_pass(ATT_SUPER // BLOCK, ATT_GROUP, load_dense, finish_dense)


def _t5_bucket(dist):
    max_exact = REL_BUCKETS // 2
    d_f = jnp.maximum(dist, 1).astype(f32)
    large = max_exact + (jnp.log(d_f / max_exact) / math.log(REL_MAX_DIST / max_exact)
                         * (REL_BUCKETS - max_exact)).astype(jnp.int32)
    large = jnp.minimum(large, REL_BUCKETS - 1)
    return jnp.where(dist < max_exact, dist, large)


def _attention_bias(rel_bias):
    i = jnp.arange(BLOCK)[:, None]
    j = jnp.arange(2 * BLOCK)[None, :]
    steps = BLOCK + i - j
    band = (steps >= 0) & (steps <= BLOCK)
    tab = rel_bias.astype(f32)
    out = []
    for _, d in DILATION_CONFIGS:
        onehot = jax.nn.one_hot(_t5_bucket(d * jnp.maximum(steps, 0)), REL_BUCKETS, dtype=f32)
        bias = jnp.einsum('qkb,bh->hqk', onehot, tab, precision=lax.Precision.HIGHEST)
        later = jnp.where(band[None], bias, NEG_INF)
        first = jnp.where((band & (j >= BLOCK))[None], bias, NEG_INF)
        out.append(jnp.stack([later, first], axis=1))
    return jnp.stack(out)


def dilated_attention(z, bias):
    b, s, _ = z.shape
    off = (2 * RET_QK_W + 2 * RET_V_W + 2 * SG_W) // ATT_HEAD_DIM

    def tile(col):
        return pl.BlockSpec((None, ATT_SUPER, ATT_HEAD_DIM), lambda bi, hd, t: (bi, t, col + hd))

    return pl.pallas_call(
        _attention_kernel,
        grid=(b, ATT_HEADS, s // ATT_SUPER),
        in_specs=[tile(off), tile(off + ATT_HEADS), tile(off + 2 * ATT_HEADS),
                  pl.BlockSpec((N_BRANCH, None, 2, BLOCK, 2 * BLOCK), lambda bi, hd, t: (0, hd, 0, 0, 0))],
        out_specs=tile(0),
        out_shape=jax.ShapeDtypeStruct((b, s, ATT_W), bf16),
        scratch_shapes=([pltpu.VMEM((2 * _padded(ATT_SUPER), ATT_HEAD_DIM), f32)] * 2
                        + [pltpu.VMEM((BLOCK + ATT_SUPER, ATT_HEAD_DIM), bf16)] * 2
                        + [pltpu.VMEM((_padded(ATT_SUPER), ATT_HEAD_DIM), f32)]
                        + [pltpu.VMEM((ATT_SUPER, ATT_HEAD_DIM), f32)]
                        + [pltpu.VMEM((_padded(ATT_SUPER), ATT_HEAD_DIM), f32)] * 3
                        + [pltpu.VMEM((ATT_SUPER, ATT_HEAD_DIM), f32)] * 3),
        compiler_params=_params(("parallel", "parallel", "arbitrary"), 32),
        name="mix_attention",
    )(z, z, z, bias)


def _merge_kernel(ret_ref, sg_ref, att_ref, gate_ref, wr_hbm, ws_hbm, wa_hbm, wo_hbm, x_ref, g_ref,
                  xo_ref, ho_ref, wr_ref, ws_ref, wa_ref, wo_ref, stage_ref, sem, *, layer):
    @pl.when(pl.program_id(0) == 0)
    def _():
        for w_hbm, w16_ref in ((wr_hbm, wr_ref), (ws_hbm, ws_ref), (wa_hbm, wa_ref), (wo_hbm, wo_ref)):
            _round_weight_into(w_hbm, layer, w16_ref, stage_ref, sem)

    d = D_MODEL
    merged = gate_ref[:, 0:d].astype(f32) * jnp.dot(ret_ref[...], wr_ref[...], preferred_element_type=f32)
    merged += gate_ref[:, d:2 * d].astype(f32) * jnp.dot(sg_ref[...], ws_ref[...], preferred_element_type=f32)
    merged += gate_ref[:, 2 * d:3 * d].astype(f32) * jnp.dot(att_ref[...], wa_ref[...], preferred_element_type=f32)
    xn = x_ref[...] + jnp.dot(merged.astype(bf16), wo_ref[...], preferred_element_type=f32)
    xo_ref[...] = xn
    ho_ref[...] = _rms(xn, g_ref[...]).astype(ho_ref.dtype)


def merge_project(ret, sg, att, gates, w_ret, w_sg, w_att, w_out, layer, x, g_next, tm=256):
    n, d = x.shape
    weights = (w_ret, w_sg, w_att, w_out)

    def rows(w):
        return pl.BlockSpec((tm, w), lambda i: (i, 0))

    return pl.pallas_call(
        functools.partial(_merge_kernel, layer=layer),
        grid=(n // tm,),
        in_specs=[rows(RET_V_W), rows(SG_W), rows(ATT_W), rows(GATE_W),
                  *[pl.BlockSpec(memory_space=pl.ANY) for _ in weights],
                  rows(d), pl.BlockSpec((1, d), lambda i: (0, 0))],
        out_specs=[rows(d), rows(d)],
        out_shape=[jax.ShapeDtypeStruct((n, d), f32), jax.ShapeDtypeStruct((n, d), bf16)],
        scratch_shapes=[*[pltpu.VMEM(w.shape[1:], bf16) for w in weights],
                        pltpu.VMEM((2, RESIDENT_STAGE_ROWS, d), f32),
                        pltpu.SemaphoreType.DMA((2,))],
        compiler_params=_params(("arbitrary",), 60),
        name="mix_merge",
    )(ret, sg, att, gates, *weights, x, g_next)


def kernel(x, ffn1_norm, ffn1_w_gate, ffn1_w_up, ffn1_w_down, mix_norm, w_in, b_gate, sg_ln_g, sg_ln_b,
           sg_w, sg_b, rel_bias, w_proj_ret, w_proj_sg, w_proj_att, w_out, ffn2_norm, ffn2_w_gate,
           ffn2_w_up, ffn2_w_down, final_norm):
    b, s, d = x.shape
    n = b * s
    depth = ffn1_norm.shape[0]

    b_gate3 = b_gate[:, None, :]
    ln_g3, ln_b3 = sg_ln_g[:, None, :], sg_ln_b[:, None, :]
    sg_b4 = sg_b[..., None]
    tables = _retention_tables()
    cos, sin = _rotary_tables(s)
    att_bias = _attention_bias(rel_bias)

    xr = x.reshape(n, d)
    h = rmsnorm(xr, ffn1_norm[0][None], bf16)
    for l in range(depth):
        a = gateup(h, ffn1_w_gate, ffn1_w_up, l)
        xr, h = down_residual_norm(a, ffn1_w_down, l, xr, mix_norm[l][None], bf16)

        z = in_projection(h, w_in, l, cos, sin)
        z3 = z.reshape(b, s, Z_W)
        ret = retention(z3, tables).reshape(n, RET_V_W)
        sg = spatial_gating(z, ln_g3, ln_b3, sg_w, sg_b4, l)
        att = dilated_attention(z3, att_bias).reshape(n, ATT_W)
        gates = in_gates(h, w_in, b_gate3, l)
        xr, h = merge_project(ret, sg, att, gates, w_proj_ret, w_proj_sg, w_proj_att, w_out, l, xr,
                              ffn2_norm[l][None])

        a = gateup(h, ffn2_w_gate, ffn2_w_up, l)
        last = l == depth - 1
        g_next = final_norm[None] if last else ffn1_norm[l + 1][None]
        xr, h = down_residual_norm(a, ffn2_w_down, l, xr, g_next, f32 if last else bf16)
    return h.reshape(b, s, d)
```

```python
import functools
import math

import jax
import jax.numpy as jnp
from jax import lax
from jax.experimental import pallas as pl
from jax.experimental.pallas import tpu as pltpu

f32 = jnp.float32
bf16 = jnp.bfloat16

D_MODEL = 2048
DEPTH = 4
BLOCK = 128
RET_HEADS = 4
RET_QK_DIM = 256
RET_V_DIM = 512
SG_GROUPS = 4
SG_GROUP_DIM = 256
ATT_HEADS = 8
ATT_HEAD_DIM = 128
DILATION_CONFIGS = ((128, 1), (512, 4), (2048, 16))
REL_BUCKETS = 32
REL_MAX_DIST = 2048
D_FF = 5632
ROPE_BASE = 10000.0
EPS = 1e-6
NEG_INF = -1e30

RET_QK_W = RET_HEADS * RET_QK_DIM
RET_V_W = RET_HEADS * RET_V_DIM
SG_W = SG_GROUPS * SG_GROUP_DIM
ATT_W = ATT_HEADS * ATT_HEAD_DIM
N_BRANCH = 3
Z_W = 2 * RET_QK_W + 2 * RET_V_W + 2 * SG_W + 3 * ATT_W
GATE_W = N_BRANCH * D_MODEL

ATT_SUPER = 2048
ATT_GROUP = 4
ATT_GROUP_FIRST = 16
ATT_PAD_EVERY = 4
MM_ROW_CHUNK = 256
RESIDENT_STAGE_ROWS = 256
V7X_VMEM_LIMIT = 60 * 1024 * 1024


def _params(sem, vmem_mb=None):
    return pltpu.CompilerParams(
        dimension_semantics=sem,
        vmem_limit_bytes=None if vmem_mb is None else min(vmem_mb * 1024 * 1024, V7X_VMEM_LIMIT))


def _rms(x, g):
    return x * lax.rsqrt(jnp.mean(x * x, axis=-1, keepdims=True) + EPS) * g


def _rmsnorm_kernel(x_ref, g_ref, o_ref):
    o_ref[...] = _rms(x_ref[...], g_ref[...]).astype(o_ref.dtype)


def rmsnorm(x, g, out_dtype, tm=512):
    n, d = x.shape
    return pl.pallas_call(
        _rmsnorm_kernel,
        grid=(n // tm,),
        in_specs=[pl.BlockSpec((tm, d), lambda i: (i, 0)),
                  pl.BlockSpec((1, d), lambda i: (0, 0))],
        out_specs=pl.BlockSpec((tm, d), lambda i: (i, 0)),
        out_shape=jax.ShapeDtypeStruct((n, d), out_dtype),
        compiler_params=_params(("parallel",)),
        name="rmsnorm",
    )(x, g)


def _row_chunks(rows):
    return [pl.ds(c * MM_ROW_CHUNK, MM_ROW_CHUNK) for c in range(rows // MM_ROW_CHUNK)]


def _gateup_kernel(h_ref, wg_ref, wu_ref, o_ref, wg16_ref, wu16_ref):
    @pl.when(pl.program_id(1) == 0)
    def _():
        wg16_ref[...] = wg_ref[...].astype(bf16)
        wu16_ref[...] = wu_ref[...].astype(bf16)

    for rows in _row_chunks(h_ref.shape[0]):
        h = h_ref[rows, :]
        g = jnp.dot(h, wg16_ref[...], preferred_element_type=f32)
        u = jnp.dot(h, wu16_ref[...], preferred_element_type=f32)
        o_ref[rows, :] = (g * jax.nn.sigmoid(g) * u).astype(o_ref.dtype)


def gateup(h, wg, wu, layer, tm=2048, tf=512):
    n, d = h.shape
    dff = wg.shape[-1]
    return pl.pallas_call(
        _gateup_kernel,
        grid=(dff // tf, n // tm),
        in_specs=[pl.BlockSpec((tm, d), lambda j, i: (i, 0)),
                  pl.BlockSpec((None, d, tf), lambda j, i: (layer, 0, j)),
                  pl.BlockSpec((None, d, tf), lambda j, i: (layer, 0, j))],
        out_specs=pl.BlockSpec((tm, tf), lambda j, i: (i, j)),
        out_shape=jax.ShapeDtypeStruct((n, dff), bf16),
        scratch_shapes=[pltpu.VMEM((d, tf), bf16), pltpu.VMEM((d, tf), bf16)],
        compiler_params=_params(("arbitrary", "arbitrary"), 56),
        name="ffn_gateup",
    )(h, wg, wu)


def _round_weight_into(w_hbm, layer, w16_ref, stage_ref, sem, on_chunk=None):
    rows = stage_ref.shape[1]
    nchunks = w16_ref.shape[0] // rows

    def copy(c):
        return pltpu.make_async_copy(w_hbm.at[layer, pl.ds(c * rows, rows), :], stage_ref.at[c % 2], sem.at[c % 2])

    copy(0).start()
    for c in range(nchunks):
        if c + 1 < nchunks:
            copy(c + 1).start()
        copy(c).wait()
        w16_ref[pl.ds(c * rows, rows), :] = stage_ref[c % 2].astype(bf16)
        if on_chunk is not None:
            on_chunk(c, rows)


def _down_kernel(a_ref, w_hbm, x_ref, g_ref, xo_ref, ho_ref, w16_ref, stage_ref, sem, *, layer):
    def finish(acc):
        xn = x_ref[...] + 0.5 * acc
        xo_ref[...] = xn
        ho_ref[...] = _rms(xn, g_ref[...]).astype(ho_ref.dtype)

    @pl.when(pl.program_id(0) == 0)
    def _():
        acc = [jnp.zeros(xo_ref.shape, f32)]

        def partial_product(c, rows):
            acc[0] += jnp.dot(a_ref[:, c * rows:(c + 1) * rows], w16_ref[pl.ds(c * rows, rows), :],
                              preferred_element_type=f32)

        _round_weight_into(w_hbm, layer, w16_ref, stage_ref, sem, partial_product)
        finish(acc[0])

    @pl.when(pl.program_id(0) > 0)
    def _():
        finish(jnp.dot(a_ref[...], w16_ref[...], preferred_element_type=f32))


def down_residual_norm(a, wd, layer, x, g_next, h_dtype, tm=256):
    n, dff = a.shape
    d = wd.shape[-1]
    return pl.pallas_call(
        functools.partial(_down_kernel, layer=layer),
        grid=(n // tm,),
        in_specs=[pl.BlockSpec((tm, dff), lambda i: (i, 0)),
                  pl.BlockSpec(memory_space=pl.ANY),
                  pl.BlockSpec((tm, d), lambda i: (i, 0)),
                  pl.BlockSpec((1, d), lambda i: (0, 0))],
        out_specs=[pl.BlockSpec((tm, d), lambda i: (i, 0)),
                   pl.BlockSpec((tm, d), lambda i: (i, 0))],
        out_shape=[jax.ShapeDtypeStruct((n, d), f32),
                   jax.ShapeDtypeStruct((n, d), h_dtype)],
        scratch_shapes=[pltpu.VMEM((dff, d), bf16),
                        pltpu.VMEM((2, RESIDENT_STAGE_ROWS, d), f32),
                        pltpu.SemaphoreType.DMA((2,))],
        compiler_params=_params(("arbitrary",), 56),
        name="ffn_down",
    )(a, wd, x, g_next)


def _stream_project(h_ref, w16_ref, o_ref, epilogue):
    for rows in _row_chunks(h_ref.shape[0]):
        acc = jnp.dot(h_ref[rows, :], w16_ref[...], preferred_element_type=f32)
        o_ref[rows, :] = epilogue(rows, acc).astype(o_ref.dtype)


def _inproj_kernel(h_ref, w_ref, cos_ref, sin_ref, o_ref, w16_ref, *, tn):
    j = pl.program_id(0)

    @pl.when(pl.program_id(1) == 0)
    def _():
        w16_ref[...] = w_ref[...].astype(bf16)

    q_tiles = RET_QK_W // tn
    rot_tiles = 2 * q_tiles
    gate_lo = (2 * RET_QK_W + RET_V_W) // tn
    gate_hi = gate_lo + RET_V_W // tn
    half = RET_QK_DIM // 2

    @pl.when(j < rot_tiles)
    def _():
        kscale = jnp.where(j >= q_tiles, RET_QK_DIM ** -0.5, 1.0).astype(f32)

        def rotate(rows, acc):
            cos = cos_ref[rows, :] * kscale
            sin = sin_ref[rows, :] * kscale
            out = []
            for hd in range(tn // RET_QK_DIM):
                t1 = acc[:, hd * RET_QK_DIM:hd * RET_QK_DIM + half]
                t2 = acc[:, hd * RET_QK_DIM + half:(hd + 1) * RET_QK_DIM]
                out += [t1 * cos - t2 * sin, t1 * sin + t2 * cos]
            return jnp.concatenate(out, axis=-1)

        _stream_project(h_ref, w16_ref, o_ref, rotate)

    @pl.when((j >= gate_lo) & (j < gate_hi))
    def _():
        _stream_project(h_ref, w16_ref, o_ref, lambda rows, acc: acc * jax.nn.sigmoid(acc))

    @pl.when(((j >= rot_tiles) & (j < gate_lo)) | (j >= gate_hi))
    def _():
        _stream_project(h_ref, w16_ref, o_ref, lambda rows, acc: acc)


def _ingate_kernel(h_ref, w_ref, b_ref, o_ref, w16_ref):
    @pl.when(pl.program_id(1) == 0)
    def _():
        w16_ref[...] = w_ref[...].astype(bf16)

    _stream_project(h_ref, w16_ref, o_ref, lambda rows, acc: jax.nn.sigmoid(acc + b_ref[...]))


def in_projection(h, w_in, layer, cos, sin, tm=2048, tn=1024):
    n, d = h.shape
    seq_tiles = cos.shape[0] // tm
    table = pl.BlockSpec((tm, RET_QK_DIM // 2), lambda j, i: (i % seq_tiles, 0))
    return pl.pallas_call(
        functools.partial(_inproj_kernel, tn=tn),
        grid=(Z_W // tn, n // tm),
        in_specs=[pl.BlockSpec((tm, d), lambda j, i: (i, 0)),
                  pl.BlockSpec((None, d, tn), lambda j, i: (layer, 0, j)),
                  table, table],
        out_specs=pl.BlockSpec((tm, tn), lambda j, i: (i, j)),
        out_shape=jax.ShapeDtypeStruct((n, Z_W), bf16),
        scratch_shapes=[pltpu.VMEM((d, tn), bf16)],
        compiler_params=_params(("arbitrary", "arbitrary"), 58),
        name="mix_inproj",
    )(h, w_in, cos, sin)


def in_gates(h, w_in, b_gate, layer, tm=2048, tn=1024):
    n, d = h.shape
    off = Z_W // tn
    return pl.pallas_call(
        _ingate_kernel,
        grid=(GATE_W // tn, n // tm),
        in_specs=[pl.BlockSpec((tm, d), lambda j, i: (i, 0)),
                  pl.BlockSpec((None, d, tn), lambda j, i: (layer, 0, j + off)),
                  pl.BlockSpec((None, 1, tn), lambda j, i: (layer, 0, j))],
        out_specs=pl.BlockSpec((tm, tn), lambda j, i: (i, j)),
        out_shape=jax.ShapeDtypeStruct((n, GATE_W), bf16),
        scratch_shapes=[pltpu.VMEM((d, tn), bf16)],
        compiler_params=_params(("arbitrary", "arbitrary"), 58),
        name="mix_ingate",
    )(h, w_in, b_gate)


def _retention_kernel(q_ref, k_ref, v_ref, g_ref, inner_ref, xi_ref, zeta_ref, gch_ref, o_ref, state_ref,
                      *, nchunk):
    @pl.when(pl.program_id(2) == 0)
    def _():
        state_ref[...] = jnp.zeros_like(state_ref)

    inner = inner_ref[...]
    xi = xi_ref[...]
    zeta = zeta_ref[...]
    gch = gch_ref[...]
    chunks = range(nchunk)
    rows = [pl.ds(c * BLOCK, BLOCK) for c in chunks]

    q = [q_ref[rows[c], :] for c in chunks]
    k = [k_ref[rows[c], :] for c in chunks]
    v = [v_ref[rows[c], :] for c in chunks]
    a = [lax.dot_general(q[c], k[c], (((1,), (1,)), ((), ())), preferred_element_type=f32) * inner
         for c in chunks]
    upd = [jnp.dot((k[c].astype(f32) * zeta).T.astype(bf16), v[c], preferred_element_type=f32)
           for c in chunks]
    local = [jnp.dot(a[c].astype(bf16), v[c], preferred_element_type=f32) for c in chunks]
    state = state_ref[...]
    carried = []
    for c in chunks:
        carried.append(jnp.dot(q[c], state.astype(bf16), preferred_element_type=f32))
        state = state * gch + upd[c]
    state_ref[...] = state
    for c in chunks:
        o = local[c] + carried[c] * xi
        on = o * lax.rsqrt(jnp.mean(o * o, axis=-1, keepdims=True) + EPS)
        o_ref[rows[c], :] = (g_ref[rows[c], :].astype(f32) * on).astype(o_ref.dtype)


def _retention_tables():
    h = jnp.arange(RET_HEADS, dtype=f32)
    log_g = jnp.log1p(-(2.0 ** (-5.0 - h)))
    idx = jnp.arange(BLOCK, dtype=f32)
    rel = idx[:, None] - idx[None, :]
    inner = jnp.where(rel[None] >= 0, jnp.exp(jnp.maximum(rel, 0.0)[None] * log_g[:, None, None]), 0.0)
    xi = jnp.exp((idx + 1.0)[None, :] * log_g[:, None])[..., None]
    zeta = jnp.exp((BLOCK - 1.0 - idx)[None, :] * log_g[:, None])[..., None]
    gch = jnp.broadcast_to(jnp.exp(BLOCK * log_g)[:, None, None], (RET_HEADS, 1, RET_V_DIM))
    return inner, xi, zeta, gch


def _rotary_tables(seq):
    inv = ROPE_BASE ** (-jnp.arange(0, RET_QK_DIM, 2, dtype=f32) / RET_QK_DIM)
    ang = jnp.arange(seq).astype(f32)[:, None] * inv[None, :]
    return jnp.cos(ang), jnp.sin(ang)


def retention(z, tables, tokens=1024):
    b, s, _ = z.shape
    inner, xi, zeta, gch = tables
    nv = 2 * RET_QK_W // RET_V_DIM
    kern = functools.partial(_retention_kernel, nchunk=tokens // BLOCK)
    return pl.pallas_call(
        kern,
        grid=(b, RET_HEADS, s // tokens),
        in_specs=[pl.BlockSpec((None, tokens, RET_QK_DIM), lambda bi, h, t: (bi, t, h)),
                  pl.BlockSpec((None, tokens, RET_QK_DIM), lambda bi, h, t: (bi, t, RET_HEADS + h)),
                  pl.BlockSpec((None, tokens, RET_V_DIM), lambda bi, h, t: (bi, t, nv + h)),
                  pl.BlockSpec((None, tokens, RET_V_DIM), lambda bi, h, t: (bi, t, nv + RET_HEADS + h)),
                  pl.BlockSpec((None, BLOCK, BLOCK), lambda bi, h, t: (h, 0, 0)),
                  pl.BlockSpec((None, BLOCK, 1), lambda bi, h, t: (h, 0, 0)),
                  pl.BlockSpec((None, BLOCK, 1), lambda bi, h, t: (h, 0, 0)),
                  pl.BlockSpec((None, 1, RET_V_DIM), lambda bi, h, t: (h, 0, 0))],
        out_specs=pl.BlockSpec((None, tokens, RET_V_DIM), lambda bi, h, t: (bi, t, h)),
        out_shape=jax.ShapeDtypeStruct((b, s, RET_V_W), bf16),
        scratch_shapes=[pltpu.VMEM((RET_QK_DIM, RET_V_DIM), f32)],
        compiler_params=_params(("parallel", "parallel", "arbitrary"), 32),
        name="mix_retention",
    )(z, z, z, z, inner, xi, zeta, gch)


def _sgu_kernel(u_ref, v_ref, lg_ref, lb_ref, w_ref, b_ref, o_ref, *, nchunk):
    u = jax.nn.gelu(u_ref[...].astype(f32))
    v = jax.nn.gelu(v_ref[...].astype(f32))
    mu = jnp.mean(v, axis=-1, keepdims=True)
    var = jnp.mean(jnp.square(v - mu), axis=-1, keepdims=True)
    vn = ((v - mu) * lax.rsqrt(var + EPS) * lg_ref[...] + lb_ref[...]).astype(bf16)
    row = lax.broadcasted_iota(jnp.int32, (BLOCK, BLOCK), 0)
    col = lax.broadcasted_iota(jnp.int32, (BLOCK, BLOCK), 1)
    causal = row >= col
    for g in range(SG_GROUPS):
        wc = jnp.where(causal, w_ref[g], 0.0).astype(bf16)
        bias = b_ref[g]
        cols = slice(g * SG_GROUP_DIM, (g + 1) * SG_GROUP_DIM)
        for c in range(nchunk):
            rows = slice(c * BLOCK, (c + 1) * BLOCK)
            mix = jnp.dot(wc, vn[rows, cols], preferred_element_type=f32) + bias
            o_ref[rows, cols] = (u[rows, cols] * mix).astype(o_ref.dtype)


def spatial_gating(z2, ln_g, ln_b, sg_w, sg_b, layer, tokens=512):
    n = z2.shape[0]
    off = (2 * RET_QK_W + 2 * RET_V_W) // SG_W
    kern = functools.partial(_sgu_kernel, nchunk=tokens // BLOCK)
    return pl.pallas_call(
        kern,
        grid=(n // tokens,),
        in_specs=[pl.BlockSpec((tokens, SG_W), lambda i: (i, off)),
                  pl.BlockSpec((tokens, SG_W), lambda i: (i, off + 1)),
                  pl.BlockSpec((None, 1, SG_W), lambda i: (layer, 0, 0)),
                  pl.BlockSpec((None, 1, SG_W), lambda i: (layer, 0, 0)),
                  pl.BlockSpec((None, SG_GROUPS, BLOCK, BLOCK), lambda i: (layer, 0, 0, 0)),
                  pl.BlockSpec((None, SG_GROUPS, BLOCK, 1), lambda i: (layer, 0, 0, 0))],
        out_specs=pl.BlockSpec((tokens, SG_W), lambda i: (i, 0)),
        out_shape=jax.ShapeDtypeStruct((n, SG_W), bf16),
        compiler_params=_params(("parallel",), 32),
        name="mix_sgu",
    )(z2, z2, ln_g, ln_b, sg_w, sg_b)


def _padded(t):
    return t + t // ATT_PAD_EVERY


def _attention_kernel(q_ref, k_ref, v_ref, bias_ref, o_ref, kp_ref, vp_ref, kd_ref, vd_ref, qp_ref, stage_ref,
                      mp_ref, lp_ref, accp_ref, mn_ref, ln_ref, accn_ref):
    sb = pl.program_id(2)
    scale = ATT_HEAD_DIM ** -0.5
    quarter = ATT_SUPER // ATT_PAD_EVERY
    super_p = _padded(ATT_SUPER)
    cur_base = lax.rem(sb, 2) * super_p
    prev_base = super_p - cur_base

    def pad_rows(dst_ref, dst_base):
        for j in range(ATT_PAD_EVERY):
            dst_ref[pl.ds(dst_base + j, quarter, stride=ATT_PAD_EVERY + 1), :] = (
                stage_ref[pl.ds(j, quarter, stride=ATT_PAD_EVERY), :])

    @pl.when(sb == 0)
    def _():
        kd_ref[pl.ds(0, BLOCK), :] = k_ref[pl.ds(0, BLOCK), :]
        vd_ref[pl.ds(0, BLOCK), :] = v_ref[pl.ds(0, BLOCK), :]

    @pl.when(sb > 0)
    def _():
        kd_ref[pl.ds(0, BLOCK), :] = kd_ref[pl.ds(ATT_SUPER, BLOCK), :]
        vd_ref[pl.ds(0, BLOCK), :] = vd_ref[pl.ds(ATT_SUPER, BLOCK), :]

    kd_ref[pl.ds(BLOCK, ATT_SUPER), :] = k_ref[...]
    vd_ref[pl.ds(BLOCK, ATT_SUPER), :] = v_ref[...]
    for src_ref, dst_ref, base in ((k_ref, kp_ref, cur_base), (v_ref, vp_ref, cur_base), (q_ref, qp_ref, 0)):
        stage_ref[...] = src_ref[...].astype(f32)
        pad_rows(dst_ref, base)

    def softmax(s):
        mb = jnp.max(s, axis=-1, keepdims=True)
        p = jnp.exp(s - mb)
        return mb, jnp.sum(p, axis=-1, keepdims=True), p.astype(bf16)

    def merged(state, mb, lb, ob):
        mb = jnp.broadcast_to(mb, (BLOCK, ATT_HEAD_DIM))
        lb = jnp.broadcast_to(lb, (BLOCK, ATT_HEAD_DIM))
        if state is None:
            return mb, lb, ob
        m_old, l_old, acc_old = state
        m_new = jnp.maximum(m_old, mb)
        wa = jnp.exp(m_old - m_new)
        wb = jnp.exp(mb - m_new)
        return m_new, wa * l_old + wb * lb, wa * acc_old + wb * ob

    def run_pass(nblocks, group, load, finish):
        def body(step, carry):
            its = [step * group + g for g in range(group)]
            ins = [load(it) for it in its]
            logit = [lax.dot_general(q, k, (((1,), (1,)), ((), ())), preferred_element_type=f32) * scale + b
                     for q, k, _, b, _ in ins]
            soft = [softmax(s) for s in logit]
            outs = [jnp.dot(soft[g][2], ins[g][2], preferred_element_type=f32) for g in range(group)]
            new = [merged(ins[g][4], soft[g][0], soft[g][1], outs[g]) for g in range(group)]
            for g in range(group):
                finish(its[g], *new[g])
            return carry

        lax.fori_loop(0, nblocks // group, body, 0)

    for bi, (window, d) in reversed(list(enumerate(DILATION_CONFIGS))[1:]):
        assert window // d == BLOCK and ATT_SUPER % (d * BLOCK) == 0 and d % ATT_PAD_EVERY == 0
        span_p = _padded(d * BLOCK)
        blocks_per_super = ATT_SUPER // (d * BLOCK)
        shift = d.bit_length() - 1
        stride = _padded(d)
        is_first_pass = bi == len(DILATION_CONFIGS) - 1

        def rows(start, stride=stride):
            return pl.ds(start, BLOCK, stride=stride)

        def load(it, bi=bi, d=d, span_p=span_p, blocks_per_super=blocks_per_super, shift=shift,
                 rows=rows, is_first_pass=is_first_pass):
            n_loc = lax.shift_right_logical(it, shift)
            r = lax.bitwise_and(it, d - 1)
            r_p = r + lax.shift_right_logical(r, 2)
            first = jnp.logical_and(sb == 0, n_loc == 0)
            q_rows = rows(n_loc * span_p + r_p)
            cur = cur_base + n_loc * span_p + r_p
            before = jnp.where(sb == 0, cur_base, prev_base + (blocks_per_super - 1) * span_p)
            prev = jnp.where(n_loc > 0, cur_base + (n_loc - 1) * span_p, before) + r_p
            k = jnp.concatenate([kp_ref[rows(prev), :], kp_ref[rows(cur), :]], axis=0).astype(bf16)
            v = jnp.concatenate([vp_ref[rows(prev), :], vp_ref[rows(cur), :]], axis=0).astype(bf16)
            state = None if is_first_pass else (mp_ref[q_rows, :], lp_ref[q_rows, :], accp_ref[q_rows, :])
            return qp_ref[q_rows, :].astype(bf16), k, v, bias_ref[bi, first.astype(jnp.int32)], state

        def finish(it, m, l, acc, bi=bi, d=d, span_p=span_p, shift=shift, rows=rows):
            n_loc = lax.shift_right_logical(it, shift)
            r = lax.bitwise_and(it, d - 1)
            if bi == 1:
                q_rows = pl.ds(n_loc * (d * BLOCK) + r, BLOCK, stride=d)
                mn_ref[q_rows, :], ln_ref[q_rows, :], accn_ref[q_rows, :] = m, l, acc
            else:
                q_rows = rows(n_loc * span_p + r + lax.shift_right_logical(r, 2))
                mp_ref[q_rows, :], lp_ref[q_rows, :], accp_ref[q_rows, :] = m, l, acc

        run_pass(blocks_per_super * d, ATT_GROUP_FIRST if is_first_pass else ATT_GROUP, load, finish)

    window, d = DILATION_CONFIGS[0]
    assert d == 1 and window == BLOCK and len(DILATION_CONFIGS) == 3

    def load_dense(it):
        q_rows = pl.ds(pl.multiple_of(it * BLOCK, BLOCK), BLOCK)
        kv_rows = pl.ds(pl.multiple_of(it * BLOCK, BLOCK), 2 * BLOCK)
        first = jnp.logical_and(sb == 0, it == 0).astype(jnp.int32)
        return (q_ref[q_rows, :], kd_ref[kv_rows, :], vd_ref[kv_rows, :], bias_ref[0, first],
                (mn_ref[q_rows, :], ln_ref[q_rows, :], accn_ref[q_rows, :]))

    def finish_dense(it, m, l, acc):
        o_ref[pl.ds(pl.multiple_of(it * BLOCK, BLOCK), BLOCK), :] = (acc / l).astype(o_ref.dtype)

    run_pass(ATT_SUPER // BLOCK, ATT_GROUP, load_dense, finish_dense)


def _t5_bucket(dist):
    max_exact = REL_BUCKETS // 2
    d_f = jnp.maximum(dist, 1).astype(f32)
    large = max_exact + (jnp.log(d_f / max_exact) / math.log(REL_MAX_DIST / max_exact)
                         * (REL_BUCKETS - max_exact)).astype(jnp.int32)
    large = jnp.minimum(large, REL_BUCKETS - 1)
    return jnp.where(dist < max_exact, dist, large)


def _attention_bias(rel_bias):
    i = jnp.arange(BLOCK)[:, None]
    j = jnp.arange(2 * BLOCK)[None, :]
    steps = BLOCK + i - j
    band = (steps >= 0) & (steps <= BLOCK)
    tab = rel_bias.astype(f32)
    out = []
    for _, d in DILATION_CONFIGS:
        onehot = jax.nn.one_hot(_t5_bucket(d * jnp.maximum(steps, 0)), REL_BUCKETS, dtype=f32)
        bias = jnp.einsum('qkb,bh->hqk', onehot, tab, precision=lax.Precision.HIGHEST)
        later = jnp.where(band[None], bias, NEG_INF)
        first = jnp.where((band & (j >= BLOCK))[None], bias, NEG_INF)
        out.append(jnp.stack([later, first], axis=1))
    return jnp.stack(out)


def dilated_attention(z, bias):
    b, s, _ = z.shape
    off = (2 * RET_QK_W + 2 * RET_V_W + 2 * SG_W) // ATT_HEAD_DIM

    def tile(col):
        return pl.BlockSpec((None, ATT_SUPER, ATT_HEAD_DIM), lambda bi, hd, t: (bi, t, col + hd))

    return pl.pallas_call(
        _attention_kernel,
        grid=(b, ATT_HEADS, s // ATT_SUPER),
        in_specs=[tile(off), tile(off + ATT_HEADS), tile(off + 2 * ATT_HEADS),
                  pl.BlockSpec((N_BRANCH, None, 2, BLOCK, 2 * BLOCK), lambda bi, hd, t: (0, hd, 0, 0, 0))],
        out_specs=tile(0),
        out_shape=jax.ShapeDtypeStruct((b, s, ATT_W), bf16),
        scratch_shapes=([pltpu.VMEM((2 * _padded(ATT_SUPER), ATT_HEAD_DIM), f32)] * 2
                        + [pltpu.VMEM((BLOCK + ATT_SUPER, ATT_HEAD_DIM), bf16)] * 2
                        + [pltpu.VMEM((_padded(ATT_SUPER), ATT_HEAD_DIM), f32)]
                        + [pltpu.VMEM((ATT_SUPER, ATT_HEAD_DIM), f32)]
                        + [pltpu.VMEM((_padded(ATT_SUPER), ATT_HEAD_DIM), f32)] * 3
                        + [pltpu.VMEM((ATT_SUPER, ATT_HEAD_DIM), f32)] * 3),
        compiler_params=_params(("parallel", "parallel", "arbitrary"), 32),
        name="mix_attention",
    )(z, z, z, bias)


def _merge_kernel(ret_ref, sg_ref, att_ref, gate_ref, wr_hbm, ws_hbm, wa_hbm, wo_hbm, x_ref, g_ref,
                  xo_ref, ho_ref, wr_ref, ws_ref, wa_ref, wo_ref, stage_ref, sem, *, layer):
    @pl.when(pl.program_id(0) == 0)
    def _():
        for w_hbm, w16_ref in ((wr_hbm, wr_ref), (ws_hbm, ws_ref), (wa_hbm, wa_ref), (wo_hbm, wo_ref)):
            _round_weight_into(w_hbm, layer, w16_ref, stage_ref, sem)

    d = D_MODEL
    merged = gate_ref[:, 0:d].astype(f32) * jnp.dot(ret_ref[...], wr_ref[...], preferred_element_type=f32)
    merged += gate_ref[:, d:2 * d].astype(f32) * jnp.dot(sg_ref[...], ws_ref[...], preferred_element_type=f32)
    merged += gate_ref[:, 2 * d:3 * d].astype(f32) * jnp.dot(att_ref[...], wa_ref[...], preferred_element_type=f32)
    xn = x_ref[...] + jnp.dot(merged.astype(bf16), wo_ref[...], preferred_element_type=f32)
    xo_ref[...] = xn
    ho_ref[...] = _rms(xn, g_ref[...]).astype(ho_ref.dtype)


def merge_project(ret, sg, att, gates, w_ret, w_sg, w_att, w_out, layer, x, g_next, tm=256):
    n, d = x.shape
    weights = (w_ret, w_sg, w_att, w_out)

    def rows(w):
        return pl.BlockSpec((tm, w), lambda i: (i, 0))

    return pl.pallas_call(
        functools.partial(_merge_kernel, layer=layer),
        grid=(n // tm,),
        in_specs=[rows(RET_V_W), rows(SG_W), rows(ATT_W), rows(GATE_W),
                  *[pl.BlockSpec(memory_space=pl.ANY) for _ in weights],
                  rows(d), pl.BlockSpec((1, d), lambda i: (0, 0))],
        out_specs=[rows(d), rows(d)],
        out_shape=[jax.ShapeDtypeStruct((n, d), f32), jax.ShapeDtypeStruct((n, d), bf16)],
        scratch_shapes=[*[pltpu.VMEM(w.shape[1:], bf16) for w in weights],
                        pltpu.VMEM((2, RESIDENT_STAGE_ROWS, d), f32),
                        pltpu.SemaphoreType.DMA((2,))],
        compiler_params=_params(("arbitrary",), 60),
        name="mix_merge",
    )(ret, sg, att, gates, *weights, x, g_next)


def kernel(x, ffn1_norm, ffn1_w_gate, ffn1_w_up, ffn1_w_down, mix_norm, w_in, b_gate, sg_ln_g, sg_ln_b,
           sg_w, sg_b, rel_bias, w_proj_ret, w_proj_sg, w_proj_att, w_out, ffn2_norm, ffn2_w_gate,
           ffn2_w_up, ffn2_w_down, final_norm):
    b, s, d = x.shape
    n = b * s
    depth = ffn1_norm.shape[0]

    b_gate3 = b_gate[:, None, :]
    ln_g3, ln_b3 = sg_ln_g[:, None, :], sg_ln_b[:, None, :]
    sg_b4 = sg_b[..., None]
    tables = _retention_tables()
    cos, sin = _rotary_tables(s)
    att_bias = _attention_bias(rel_bias)

    xr = x.reshape(n, d)
    h = rmsnorm(xr, ffn1_norm[0][None], bf16)
    for l in range(depth):
        a = gateup(h, ffn1_w_gate, ffn1_w_up, l)
        xr, h = down_residual_norm(a, ffn1_w_down, l, xr, mix_norm[l][None], bf16)

        z = in_projection(h, w_in, l, cos, sin)
        z3 = z.reshape(b, s, Z_W)
        ret = retention(z3, tables).reshape(n, RET_V_W)
        sg = spatial_gating(z, ln_g3, ln_b3, sg_w, sg_b4, l)
        att = dilated_attention(z3, att_bias).reshape(n, ATT_W)
        gates = in_gates(h, w_in, b_gate3, l)
        xr, h = merge_project(ret, sg, att, gates, w_proj_ret, w_proj_sg, w_proj_att, w_out, l, xr,
                              ffn2_norm[l][None])

        a = gateup(h, ffn2_w_gate, ffn2_w_up, l)
        last = l == depth - 1
        g_next = final_norm[None] if last else ffn1_norm[l + 1][None]
        xr, h = down_residual_norm(a, ffn2_w_down, l, xr, g_next, f32 if last else bf16)
    return h.reshape(b, s, d)
```

```python
import functools
import math

import jax
import jax.numpy as jnp
from jax import lax
from jax.experimental import pallas as pl
from jax.experimental.pallas import tpu as pltpu

f32 = jnp.float32
bf16 = jnp.bfloat16

D_MODEL = 2048
DEPTH = 4
BLOCK = 128
RET_HEADS = 4
RET_QK_DIM = 256
RET_V_DIM = 512
SG_GROUPS = 4
SG_GROUP_DIM = 256
ATT_HEADS = 8
ATT_HEAD_DIM = 128
DILATION_CONFIGS = ((128, 1), (512, 4), (2048, 16))
REL_BUCKETS = 32
REL_MAX_DIST = 2048
D_FF = 5632
ROPE_BASE = 10000.0
EPS = 1e-6
NEG_INF = -1e30

RET_QK_W = RET_HEADS * RET_QK_DIM
RET_V_W = RET_HEADS * RET_V_DIM
SG_W = SG_GROUPS * SG_GROUP_DIM
ATT_W = ATT_HEADS * ATT_HEAD_DIM
N_BRANCH = 3
Z_W = 2 * RET_QK_W + 2 * RET_V_W + 2 * SG_W + 3 * ATT_W
GATE_W = N_BRANCH * D_MODEL

ATT_SUPER = 2048
ATT_GROUP = 4
ATT_GROUP_FIRST = 16
ATT_PAD_EVERY = 4
MM_ROW_CHUNK = 256
RESIDENT_STAGE_ROWS = 256
V7X_VMEM_LIMIT = 60 * 1024 * 1024


def _params(sem, vmem_mb=None):
    return pltpu.CompilerParams(
        dimension_semantics=sem,
        vmem_limit_bytes=None if vmem_mb is None else min(vmem_mb * 1024 * 1024, V7X_VMEM_LIMIT))


def _rms(x, g):
    return x * lax.rsqrt(jnp.mean(x * x, axis=-1, keepdims=True) + EPS) * g


def _rmsnorm_kernel(x_ref, g_ref, o_ref):
    o_ref[...] = _rms(x_ref[...], g_ref[...]).astype(o_ref.dtype)


def rmsnorm(x, g, out_dtype, tm=512):
    n, d = x.shape
    return pl.pallas_call(
        _rmsnorm_kernel,
        grid=(n // tm,),
        in_specs=[pl.BlockSpec((tm, d), lambda i: (i, 0)),
                  pl.BlockSpec((1, d), lambda i: (0, 0))],
        out_specs=pl.BlockSpec((tm, d), lambda i: (i, 0)),
        out_shape=jax.ShapeDtypeStruct((n, d), out_dtype),
        compiler_params=_params(("parallel",)),
        name="rmsnorm",
    )(x, g)


def _row_chunks(rows):
    return [pl.ds(c * MM_ROW_CHUNK, MM_ROW_CHUNK) for c in range(rows // MM_ROW_CHUNK)]


def _gateup_kernel(h_ref, wg_ref, wu_ref, o_ref, wg16_ref, wu16_ref):
    @pl.when(pl.program_id(1) == 0)
    def _():
        wg16_ref[...] = wg_ref[...].astype(bf16)
        wu16_ref[...] = wu_ref[...].astype(bf16)

    for rows in _row_chunks(h_ref.shape[0]):
        h = h_ref[rows, :]
        g = jnp.dot(h, wg16_ref[...], preferred_element_type=f32)
        u = jnp.dot(h, wu16_ref[...], preferred_element_type=f32)
        o_ref[rows, :] = (g * jax.nn.sigmoid(g) * u).astype(o_ref.dtype)


def gateup(h, wg, wu, layer, tm=2048, tf=512):
    n, d = h.shape
    dff = wg.shape[-1]
    return pl.pallas_call(
        _gateup_kernel,
        grid=(dff // tf, n // tm),
        in_specs=[pl.BlockSpec((tm, d), lambda j, i: (i, 0)),
                  pl.BlockSpec((None, d, tf), lambda j, i: (layer, 0, j)),
                  pl.BlockSpec((None, d, tf), lambda j, i: (layer, 0, j))],
        out_specs=pl.BlockSpec((tm, tf), lambda j, i: (i, j)),
        out_shape=jax.ShapeDtypeStruct((n, dff), bf16),
        scratch_shapes=[pltpu.VMEM((d, tf), bf16), pltpu.VMEM((d, tf), bf16)],
        compiler_params=_params(("arbitrary", "arbitrary"), 56),
        name="ffn_gateup",
    )(h, wg, wu)


def _round_weight_into(w_hbm, layer, w16_ref, stage_ref, sem, on_chunk=None):
    rows = stage_ref.shape[1]
    nchunks = w16_ref.shape[0] // rows

    def copy(c):
        return pltpu.make_async_copy(w_hbm.at[layer, pl.ds(c * rows, rows), :], stage_ref.at[c % 2], sem.at[c % 2])

    copy(0).start()
    for c in range(nchunks):
        if c + 1 < nchunks:
            copy(c + 1).start()
        copy(c).wait()
        w16_ref[pl.ds(c * rows, rows), :] = stage_ref[c % 2].astype(bf16)
        if on_chunk is not None:
            on_chunk(c, rows)


def _down_kernel(a_ref, w_hbm, x_ref, g_ref, xo_ref, ho_ref, w16_ref, stage_ref, sem, *, layer):
    def finish(acc):
        xn = x_ref[...] + 0.5 * acc
        xo_ref[...] = xn
        ho_ref[...] = _rms(xn, g_ref[...]).astype(ho_ref.dtype)

    @pl.when(pl.program_id(0) == 0)
    def _():
        acc = [jnp.zeros(xo_ref.shape, f32)]

        def partial_product(c, rows):
            acc[0] += jnp.dot(a_ref[:, c * rows:(c + 1) * rows], w16_ref[pl.ds(c * rows, rows), :],
                              preferred_element_type=f32)

        _round_weight_into(w_hbm, layer, w16_ref, stage_ref, sem, partial_product)
        finish(acc[0])

    @pl.when(pl.program_id(0) > 0)
    def _():
        finish(jnp.dot(a_ref[...], w16_ref[...], preferred_element_type=f32))


def down_residual_norm(a, wd, layer, x, g_next, h_dtype, tm=256):
    n, dff = a.shape
    d = wd.shape[-1]
    return pl.pallas_call(
        functools.partial(_down_kernel, layer=layer),
        grid=(n // tm,),
        in_specs=[pl.BlockSpec((tm, dff), lambda i: (i, 0)),
                  pl.BlockSpec(memory_space=pl.ANY),
                  pl.BlockSpec((tm, d), lambda i: (i, 0)),
                  pl.BlockSpec((1, d), lambda i: (0, 0))],
        out_specs=[pl.BlockSpec((tm, d), lambda i: (i, 0)),
                   pl.BlockSpec((tm, d), lambda i: (i, 0))],
        out_shape=[jax.ShapeDtypeStruct((n, d), f32),
                   jax.ShapeDtypeStruct((n, d), h_dtype)],
        scratch_shapes=[pltpu.VMEM((dff, d), bf16),
                        pltpu.VMEM((2, RESIDENT_STAGE_ROWS, d), f32),
                        pltpu.SemaphoreType.DMA((2,))],
        compiler_params=_params(("arbitrary",), 56),
        name="ffn_down",
    )(a, wd, x, g_next)


def _stream_project(h_ref, w16_ref, o_ref, epilogue):
    for rows in _row_chunks(h_ref.shape[0]):
        acc = jnp.dot(h_ref[rows, :], w16_ref[...], preferred_element_type=f32)
        o_ref[rows, :] = epilogue(rows, acc).astype(o_ref.dtype)


def _inproj_kernel(h_ref, w_ref, cos_ref, sin_ref, o_ref, w16_ref, *, tn):
    j = pl.program_id(0)

    @pl.when(pl.program_id(1) == 0)
    def _():
        w16_ref[...] = w_ref[...].astype(bf16)

    q_tiles = RET_QK_W // tn
    rot_tiles = 2 * q_tiles
    gate_lo = (2 * RET_QK_W + RET_V_W) // tn
    gate_hi = gate_lo + RET_V_W // tn
    half = RET_QK_DIM // 2

    @pl.when(j < rot_tiles)
    def _():
        kscale = jnp.where(j >= q_tiles, RET_QK_DIM ** -0.5, 1.0).astype(f32)

        def rotate(rows, acc):
            cos = cos_ref[rows, :] * kscale
            sin = sin_ref[rows, :] * kscale
            out = []
            for hd in range(tn // RET_QK_DIM):
                t1 = acc[:, hd * RET_QK_DIM:hd * RET_QK_DIM + half]
                t2 = acc[:, hd * RET_QK_DIM + half:(hd + 1) * RET_QK_DIM]
                out += [t1 * cos - t2 * sin, t1 * sin + t2 * cos]
            return jnp.concatenate(out, axis=-1)

        _stream_project(h_ref, w16_ref, o_ref, rotate)

    @pl.when((j >= gate_lo) & (j < gate_hi))
    def _():
        _stream_project(h_ref, w16_ref, o_ref, lambda rows, acc: acc * jax.nn.sigmoid(acc))

    @pl.when(((j >= rot_tiles) & (j < gate_lo)) | (j >= gate_hi))
    def _():
        _stream_project(h_ref, w16_ref, o_ref, lambda rows, acc: acc)


def _ingate_kernel(h_ref, w_ref, b_ref, o_ref, w16_ref):
    @pl.when(pl.program_id(1) == 0)
    def _():
        w16_ref[...] = w_ref[...].astype(bf16)

    _stream_project(h_ref, w16_ref, o_ref, lambda rows, acc: jax.nn.sigmoid(acc + b_ref[...]))


def in_projection(h, w_in, layer, cos, sin, tm=2048, tn=1024):
    n, d = h.shape
    seq_tiles = cos.shape[0] // tm
    table = pl.BlockSpec((tm, RET_QK_DIM // 2), lambda j, i: (i % seq_tiles, 0))
    return pl.pallas_call(
        functools.partial(_inproj_kernel, tn=tn),
        grid=(Z_W // tn, n // tm),
        in_specs=[pl.BlockSpec((tm, d), lambda j, i: (i, 0)),
                  pl.BlockSpec((None, d, tn), lambda j, i: (layer, 0, j)),
                  table, table],
        out_specs=pl.BlockSpec((tm, tn), lambda j, i: (i, j)),
        out_shape=jax.ShapeDtypeStruct((n, Z_W), bf16),
        scratch_shapes=[pltpu.VMEM((d, tn), bf16)],
        compiler_params=_params(("arbitrary", "arbitrary"), 58),
        name="mix_inproj",
    )(h, w_in, cos, sin)


def in_gates(h, w_in, b_gate, layer, tm=2048, tn=1024):
    n, d = h.shape
    off = Z_W // tn
    return pl.pallas_call(
        _ingate_kernel,
        grid=(GATE_W // tn, n // tm),
        in_specs=[pl.BlockSpec((tm, d), lambda j, i: (i, 0)),
                  pl.BlockSpec((None, d, tn), lambda j, i: (layer, 0, j + off)),
                  pl.BlockSpec((None, 1, tn), lambda j, i: (layer, 0, j))],
        out_specs=pl.BlockSpec((tm, tn), lambda j, i: (i, j)),
        out_shape=jax.ShapeDtypeStruct((n, GATE_W), bf16),
        scratch_shapes=[pltpu.VMEM((d, tn), bf16)],
        compiler_params=_params(("arbitrary", "arbitrary"), 58),
        name="mix_ingate",
    )(h, w_in, b_gate)


def _retention_kernel(q_ref, k_ref, v_ref, g_ref, inner_ref, xi_ref, zeta_ref, gch_ref, o_ref, state_ref,
                      *, nchunk):
    @pl.when(pl.program_id(2) == 0)
    def _():
        state_ref[...] = jnp.zeros_like(state_ref)

    inner = inner_ref[...]
    xi = xi_ref[...]
    zeta = zeta_ref[...]
    gch = gch_ref[...]
    chunks = range(nchunk)
    rows = [pl.ds(c * BLOCK, BLOCK) for c in chunks]

    q = [q_ref[rows[c], :] for c in chunks]
    k = [k_ref[rows[c], :] for c in chunks]
    v = [v_ref[rows[c], :] for c in chunks]
    a = [lax.dot_general(q[c], k[c], (((1,), (1,)), ((), ())), preferred_element_type=f32) * inner
         for c in chunks]
    upd = [jnp.dot((k[c].astype(f32) * zeta).T.astype(bf16), v[c], preferred_element_type=f32)
           for c in chunks]
    local = [jnp.dot(a[c].astype(bf16), v[c], preferred_element_type=f32) for c in chunks]
    state = state_ref[...]
    carried = []
    for c in chunks:
        carried.append(jnp.dot(q[c], state.astype(bf16), preferred_element_type=f32))
        state = state * gch + upd[c]
    state_ref[...] = state
    for c in chunks:
        o = local[c] + carried[c] * xi
        on = o * lax.rsqrt(jnp.mean(o * o, axis=-1, keepdims=True) + EPS)
        o_ref[rows[c], :] = (g_ref[rows[c], :].astype(f32) * on).astype(o_ref.dtype)


def _retention_tables():
    h = jnp.arange(RET_HEADS, dtype=f32)
    log_g = jnp.log1p(-(2.0 ** (-5.0 - h)))
    idx = jnp.arange(BLOCK, dtype=f32)
    rel = idx[:, None] - idx[None, :]
    inner = jnp.where(rel[None] >= 0, jnp.exp(jnp.maximum(rel, 0.0)[None] * log_g[:, None, None]), 0.0)
    xi = jnp.exp((idx + 1.0)[None, :] * log_g[:, None])[..., None]
    zeta = jnp.exp((BLOCK - 1.0 - idx)[None, :] * log_g[:, None])[..., None]
    gch = jnp.broadcast_to(jnp.exp(BLOCK * log_g)[:, None, None], (RET_HEADS, 1, RET_V_DIM))
    return inner, xi, zeta, gch


def _rotary_tables(seq):
    inv = ROPE_BASE ** (-jnp.arange(0, RET_QK_DIM, 2, dtype=f32) / RET_QK_DIM)
    ang = jnp.arange(seq).astype(f32)[:, None] * inv[None, :]
    return jnp.cos(ang), jnp.sin(ang)


def retention(z, tables, tokens=2048):
    b, s, _ = z.shape
    inner, xi, zeta, gch = tables
    nv = 2 * RET_QK_W // RET_V_DIM
    kern = functools.partial(_retention_kernel, nchunk=tokens // BLOCK)
    return pl.pallas_call(
        kern,
        grid=(b, RET_HEADS, s // tokens),
        in_specs=[pl.BlockSpec((None, tokens, RET_QK_DIM), lambda bi, h, t: (bi, t, h)),
                  pl.BlockSpec((None, tokens, RET_QK_DIM), lambda bi, h, t: (bi, t, RET_HEADS + h)),
                  pl.BlockSpec((None, tokens, RET_V_DIM), lambda bi, h, t: (bi, t, nv + h)),
                  pl.BlockSpec((None, tokens, RET_V_DIM), lambda bi, h, t: (bi, t, nv + RET_HEADS + h)),
                  pl.BlockSpec((None, BLOCK, BLOCK), lambda bi, h, t: (h, 0, 0)),
                  pl.BlockSpec((None, BLOCK, 1), lambda bi, h, t: (h, 0, 0)),
                  pl.BlockSpec((None, BLOCK, 1), lambda bi, h, t: (h, 0, 0)),
                  pl.BlockSpec((None, 1, RET_V_DIM), lambda bi, h, t: (h, 0, 0))],
        out_specs=pl.BlockSpec((None, tokens, RET_V_DIM), lambda bi, h, t: (bi, t, h)),
        out_shape=jax.ShapeDtypeStruct((b, s, RET_V_W), bf16),
        scratch_shapes=[pltpu.VMEM((RET_QK_DIM, RET_V_DIM), f32)],
        compiler_params=_params(("parallel", "parallel", "arbitrary"), 32),
        name="mix_retention",
    )(z, z, z, z, inner, xi, zeta, gch)


def _sgu_kernel(u_ref, v_ref, lg_ref, lb_ref, w_ref, b_ref, o_ref, *, nchunk):
    u = jax.nn.gelu(u_ref[...].astype(f32))
    v = jax.nn.gelu(v_ref[...].astype(f32))
    mu = jnp.mean(v, axis=-1, keepdims=True)
    var = jnp.mean(jnp.square(v - mu), axis=-1, keepdims=True)
    vn = ((v - mu) * lax.rsqrt(var + EPS) * lg_ref[...] + lb_ref[...]).astype(bf16)
    row = lax.broadcasted_iota(jnp.int32, (BLOCK, BLOCK), 0)
    col = lax.broadcasted_iota(jnp.int32, (BLOCK, BLOCK), 1)
    causal = row >= col
    for g in range(SG_GROUPS):
        wc = jnp.where(causal, w_ref[g], 0.0).astype(bf16)
        bias = b_ref[g]
        cols = slice(g * SG_GROUP_DIM, (g + 1) * SG_GROUP_DIM)
        for c in range(nchunk):
            rows = slice(c * BLOCK, (c + 1) * BLOCK)
            mix = jnp.dot(wc, vn[rows, cols], preferred_element_type=f32) + bias
            o_ref[rows, cols] = (u[rows, cols] * mix).astype(o_ref.dtype)


def spatial_gating(z2, ln_g, ln_b, sg_w, sg_b, layer, tokens=1024):
    n = z2.shape[0]
    off = (2 * RET_QK_W + 2 * RET_V_W) // SG_W
    kern = functools.partial(_sgu_kernel, nchunk=tokens // BLOCK)
    return pl.pallas_call(
        kern,
        grid=(n // tokens,),
        in_specs=[pl.BlockSpec((tokens, SG_W), lambda i: (i, off)),
                  pl.BlockSpec((tokens, SG_W), lambda i: (i, off + 1)),
                  pl.BlockSpec((None, 1, SG_W), lambda i: (layer, 0, 0)),
                  pl.BlockSpec((None, 1, SG_W), lambda i: (layer, 0, 0)),
                  pl.BlockSpec((None, SG_GROUPS, BLOCK, BLOCK), lambda i: (layer, 0, 0, 0)),
                  pl.BlockSpec((None, SG_GROUPS, BLOCK, 1), lambda i: (layer, 0, 0, 0))],
        out_specs=pl.BlockSpec((tokens, SG_W), lambda i: (i, 0)),
        out_shape=jax.ShapeDtypeStruct((n, SG_W), bf16),
        compiler_params=_params(("parallel",), 32),
        name="mix_sgu",
    )(z2, z2, ln_g, ln_b, sg_w, sg_b)


def _padded(t):
    return t + t // ATT_PAD_EVERY


def _attention_kernel(q_ref, k_ref, v_ref, bias_ref, o_ref, kp_ref, vp_ref, kd_ref, vd_ref, qp_ref, stage_ref,
                      mp_ref, lp_ref, accp_ref, mn_ref, ln_ref, accn_ref):
    sb = pl.program_id(2)
    scale = ATT_HEAD_DIM ** -0.5
    quarter = ATT_SUPER // ATT_PAD_EVERY
    super_p = _padded(ATT_SUPER)
    cur_base = lax.rem(sb, 2) * super_p
    prev_base = super_p - cur_base

    def pad_rows(dst_ref, dst_base):
        for j in range(ATT_PAD_EVERY):
            dst_ref[pl.ds(dst_base + j, quarter, stride=ATT_PAD_EVERY + 1), :] = (
                stage_ref[pl.ds(j, quarter, stride=ATT_PAD_EVERY), :])

    @pl.when(sb == 0)
    def _():
        kd_ref[pl.ds(0, BLOCK), :] = k_ref[pl.ds(0, BLOCK), :]
        vd_ref[pl.ds(0, BLOCK), :] = v_ref[pl.ds(0, BLOCK), :]

    @pl.when(sb > 0)
    def _():
        kd_ref[pl.ds(0, BLOCK), :] = kd_ref[pl.ds(ATT_SUPER, BLOCK), :]
        vd_ref[pl.ds(0, BLOCK), :] = vd_ref[pl.ds(ATT_SUPER, BLOCK), :]

    kd_ref[pl.ds(BLOCK, ATT_SUPER), :] = k_ref[...]
    vd_ref[pl.ds(BLOCK, ATT_SUPER), :] = v_ref[...]
    for src_ref, dst_ref, base in ((k_ref, kp_ref, cur_base), (v_ref, vp_ref, cur_base), (q_ref, qp_ref, 0)):
        stage_ref[...] = src_ref[...].astype(f32)
        pad_rows(dst_ref, base)

    def softmax(s):
        mb = jnp.max(s, axis=-1, keepdims=True)
        p = jnp.exp(s - mb)
        return mb, jnp.sum(p, axis=-1, keepdims=True), p.astype(bf16)

    def merged(state, mb, lb, ob):
        mb = jnp.broadcast_to(mb, (BLOCK, ATT_HEAD_DIM))
        lb = jnp.broadcast_to(lb, (BLOCK, ATT_HEAD_DIM))
        if state is None:
            return mb, lb, ob
        m_old, l_old, acc_old = state
        m_new = jnp.maximum(m_old, mb)
        wa = jnp.exp(m_old - m_new)
        wb = jnp.exp(mb - m_new)
        return m_new, wa * l_old + wb * lb, wa * acc_old + wb * ob

    def run_pass(nblocks, group, load, finish):
        def body(step, carry):
            its = [step * group + g for g in range(group)]
            ins = [load(it) for it in its]
            logit = [lax.dot_general(q, k, (((1,), (1,)), ((), ())), preferred_element_type=f32) * scale + b
                     for q, k, _, b, _ in ins]
            soft = [softmax(s) for s in logit]
            outs = [jnp.dot(soft[g][2], ins[g][2], preferred_element_type=f32) for g in range(group)]
            new = [merged(ins[g][4], soft[g][0], soft[g][1], outs[g]) for g in range(group)]
            for g in range(group):
                finish(its[g], *new[g])
            return carry

        lax.fori_loop(0, nblocks // group, body, 0)

    for bi, (window, d) in reversed(list(enumerate(DILATION_CONFIGS))[1:]):
        assert window // d == BLOCK and ATT_SUPER % (d * BLOCK) == 0 and d % ATT_PAD_EVERY == 0
        span_p = _padded(d * BLOCK)
        blocks_per_super = ATT_SUPER // (d * BLOCK)
        shift = d.bit_length() - 1
        stride = _padded(d)
        is_first_pass = bi == len(DILATION_CONFIGS) - 1

        def rows(start, stride=stride):
            return pl.ds(start, BLOCK, stride=stride)

        def load(it, bi=bi, d=d, span_p=span_p, blocks_per_super=blocks_per_super, shift=shift,
                 rows=rows, is_first_pass=is_first_pass):
            n_loc = lax.shift_right_logical(it, shift)
            r = lax.bitwise_and(it, d - 1)
            r_p = r + lax.shift_right_logical(r, 2)
            first = jnp.logical_and(sb == 0, n_loc == 0)
            q_rows = rows(n_loc * span_p + r_p)
            cur = cur_base + n_loc * span_p + r_p
            before = jnp.where(sb == 0, cur_base, prev_base + (blocks_per_super - 1) * span_p)
            prev = jnp.where(n_loc > 0, cur_base + (n_loc - 1) * span_p, before) + r_p
            k = jnp.concatenate([kp_ref[rows(prev), :], kp_ref[rows(cur), :]], axis=0).astype(bf16)
            v = jnp.concatenate([vp_ref[rows(prev), :], vp_ref[rows(cur), :]], axis=0).astype(bf16)
            state = None if is_first_pass else (mp_ref[q_rows, :], lp_ref[q_rows, :], accp_ref[q_rows, :])
            return qp_ref[q_rows, :].astype(bf16), k, v, bias_ref[bi, first.astype(jnp.int32)], state

        def finish(it, m, l, acc, bi=bi, d=d, span_p=span_p, shift=shift, rows=rows):
            n_loc = lax.shift_right_logical(it, shift)
            r = lax.bitwise_and(it, d - 1)
            if bi == 1:
                q_rows = pl.ds(n_loc * (d * BLOCK) + r, BLOCK, stride=d)
                mn_ref[q_rows, :], ln_ref[q_rows, :], accn_ref[q_rows, :] = m, l, acc
            else:
                q_rows = rows(n_loc * span_p + r + lax.shift_right_logical(r, 2))
                mp_ref[q_rows, :], lp_ref[q_rows, :], accp_ref[q_rows, :] = m, l, acc

        run_pass(blocks_per_super * d, ATT_GROUP_FIRST if is_first_pass else ATT_GROUP, load, finish)

    window, d = DILATION_CONFIGS[0]
    assert d == 1 and window == BLOCK and len(DILATION_CONFIGS) == 3

    def load_dense(it):
        q_rows = pl.ds(pl.multiple_of(it * BLOCK, BLOCK), BLOCK)
        kv_rows = pl.ds(pl.multiple_of(it * BLOCK, BLOCK), 2 * BLOCK)
        first = jnp.logical_and(sb == 0, it == 0).astype(jnp.int32)
        return (q_ref[q_rows, :], kd_ref[kv_rows, :], vd_ref[kv_rows, :], bias_ref[0, first],
                (mn_ref[q_rows, :], ln_ref[q_rows, :], accn_ref[q_rows, :]))

    def finish_dense(it, m, l, acc):
        o_ref[pl.ds(pl.multiple_of(it * BLOCK, BLOCK), BLOCK), :] = (acc / l).astype(o_ref.dtype)

    run_pass(ATT_SUPER // BLOCK, ATT_GROUP, load_dense, finish_dense)


def _t5_bucket(dist):
    max_exact = REL_BUCKETS // 2
    d_f = jnp.maximum(dist, 1).astype(f32)
    large = max_exact + (jnp.log(d_f / max_exact) / math.log(REL_MAX_DIST / max_exact)
                         * (REL_BUCKETS - max_exact)).astype(jnp.int32)
    large = jnp.minimum(large, REL_BUCKETS - 1)
    return jnp.where(dist < max_exact, dist, large)


def _attention_bias(rel_bias):
    i = jnp.arange(BLOCK)[:, None]
    j = jnp.arange(2 * BLOCK)[None, :]
    steps = BLOCK + i - j
    band = (steps >= 0) & (steps <= BLOCK)
    tab = rel_bias.astype(f32)
    out = []
    for _, d in DILATION_CONFIGS:
        onehot = jax.nn.one_hot(_t5_bucket(d * jnp.maximum(steps, 0)), REL_BUCKETS, dtype=f32)
        bias = jnp.einsum('qkb,bh->hqk', onehot, tab, precision=lax.Precision.HIGHEST)
        later = jnp.where(band[None], bias, NEG_INF)
        first = jnp.where((band & (j >= BLOCK))[None], bias, NEG_INF)
        out.append(jnp.stack([later, first], axis=1))
    return jnp.stack(out)


def dilated_attention(z, bias):
    b, s, _ = z.shape
    off = (2 * RET_QK_W + 2 * RET_V_W + 2 * SG_W) // ATT_HEAD_DIM

    def tile(col):
        return pl.BlockSpec((None, ATT_SUPER, ATT_HEAD_DIM), lambda bi, hd, t: (bi, t, col + hd))

    return pl.pallas_call(
        _attention_kernel,
        grid=(b, ATT_HEADS, s // ATT_SUPER),
        in_specs=[tile(off), tile(off + ATT_HEADS), tile(off + 2 * ATT_HEADS),
                  pl.BlockSpec((N_BRANCH, None, 2, BLOCK, 2 * BLOCK), lambda bi, hd, t: (0, hd, 0, 0, 0))],
        out_specs=tile(0),
        out_shape=jax.ShapeDtypeStruct((b, s, ATT_W), bf16),
        scratch_shapes=([pltpu.VMEM((2 * _padded(ATT_SUPER), ATT_HEAD_DIM), f32)] * 2
                        + [pltpu.VMEM((BLOCK + ATT_SUPER, ATT_HEAD_DIM), bf16)] * 2
                        + [pltpu.VMEM((_padded(ATT_SUPER), ATT_HEAD_DIM), f32)]
                        + [pltpu.VMEM((ATT_SUPER, ATT_HEAD_DIM), f32)]
                        + [pltpu.VMEM((_padded(ATT_SUPER), ATT_HEAD_DIM), f32)] * 3
                        + [pltpu.VMEM((ATT_SUPER, ATT_HEAD_DIM), f32)] * 3),
        compiler_params=_params(("parallel", "parallel", "arbitrary"), 32),
        name="mix_attention",
    )(z, z, z, bias)


def _merge_kernel(ret_ref, sg_ref, att_ref, gate_ref, wr_hbm, ws_hbm, wa_hbm, wo_hbm, x_ref, g_ref,
                  xo_ref, ho_ref, wr_ref, ws_ref, wa_ref, wo_ref, stage_ref, sem, *, layer):
    @pl.when(pl.program_id(0) == 0)
    def _():
        for w_hbm, w16_ref in ((wr_hbm, wr_ref), (ws_hbm, ws_ref), (wa_hbm, wa_ref), (wo_hbm, wo_ref)):
            _round_weight_into(w_hbm, layer, w16_ref, stage_ref, sem)

    d = D_MODEL
    merged = gate_ref[:, 0:d].astype(f32) * jnp.dot(ret_ref[...], wr_ref[...], preferred_element_type=f32)
    merged += gate_ref[:, d:2 * d].astype(f32) * jnp.dot(sg_ref[...], ws_ref[...], preferred_element_type=f32)
    merged += gate_ref[:, 2 * d:3 * d].astype(f32) * jnp.dot(att_ref[...], wa_ref[...], preferred_element_type=f32)
    xn = x_ref[...] + jnp.dot(merged.astype(bf16), wo_ref[...], preferred_element_type=f32)
    xo_ref[...] = xn
    ho_ref[...] = _rms(xn, g_ref[...]).astype(ho_ref.dtype)


def merge_project(ret, sg, att, gates, w_ret, w_sg, w_att, w_out, layer, x, g_next, tm=256):
    n, d = x.shape
    weights = (w_ret, w_sg, w_att, w_out)

    def rows(w):
        return pl.BlockSpec((tm, w), lambda i: (i, 0))

    return pl.pallas_call(
        functools.partial(_merge_kernel, layer=layer),
        grid=(n // tm,),
        in_specs=[rows(RET_V_W), rows(SG_W), rows(ATT_W), rows(GATE_W),
                  *[pl.BlockSpec(memory_space=pl.ANY) for _ in weights],
                  rows(d), pl.BlockSpec((1, d), lambda i: (0, 0))],
        out_specs=[rows(d), rows(d)],
        out_shape=[jax.ShapeDtypeStruct((n, d), f32), jax.ShapeDtypeStruct((n, d), bf16)],
        scratch_shapes=[*[pltpu.VMEM(w.shape[1:], bf16) for w in weights],
                        pltpu.VMEM((2, RESIDENT_STAGE_ROWS, d), f32),
                        pltpu.SemaphoreType.DMA((2,))],
        compiler_params=_params(("arbitrary",), 60),
        name="mix_merge",
    )(ret, sg, att, gates, *weights, x, g_next)


def kernel(x, ffn1_norm, ffn1_w_gate, ffn1_w_up, ffn1_w_down, mix_norm, w_in, b_gate, sg_ln_g, sg_ln_b,
           sg_w, sg_b, rel_bias, w_proj_ret, w_proj_sg, w_proj_att, w_out, ffn2_norm, ffn2_w_gate,
           ffn2_w_up, ffn2_w_down, final_norm):
    b, s, d = x.shape
    n = b * s
    depth = ffn1_norm.shape[0]

    b_gate3 = b_gate[:, None, :]
    ln_g3, ln_b3 = sg_ln_g[:, None, :], sg_ln_b[:, None, :]
    sg_b4 = sg_b[..., None]
    tables = _retention_tables()
    cos, sin = _rotary_tables(s)
    att_bias = _attention_bias(rel_bias)

    xr = x.reshape(n, d)
    h = rmsnorm(xr, ffn1_norm[0][None], bf16)
    for l in range(depth):
        a = gateup(h, ffn1_w_gate, ffn1_w_up, l)
        xr, h = down_residual_norm(a, ffn1_w_down, l, xr, mix_norm[l][None], bf16)

        z = in_projection(h, w_in, l, cos, sin)
        z3 = z.reshape(b, s, Z_W)
        ret = retention(z3, tables).reshape(n, RET_V_W)
        sg = spatial_gating(z, ln_g3, ln_b3, sg_w, sg_b4, l)
        att = dilated_attention(z3, att_bias).reshape(n, ATT_W)
        gates = in_gates(h, w_in, b_gate3, l)
        xr, h = merge_project(ret, sg, att, gates, w_proj_ret, w_proj_sg, w_proj_att, w_out, l, xr,
                              ffn2_norm[l][None])

        a = gateup(h, ffn2_w_gate, ffn2_w_up, l)
        last = l == depth - 1
        g_next = final_norm[None] if last else ffn1_norm[l + 1][None]
        xr, h = down_residual_norm(a, ffn2_w_down, l, xr, g_next, f32 if last else bf16)
    return h.reshape(b, s, d)
```
